```python
import math
import jax, jax.numpy as jnp
from jax import lax
import numpy as np

D_MODEL = 2048
BATCH = 2
SEQ = 4096
DEPTH = 1

PLE_DIM = 256
N_HEADS = 16
HEAD_DIM = 128
D_ATTN = N_HEADS * HEAD_DIM
D_RNN = ((4 * D_MODEL // 3 + 255) // 256) * 256
N_RNN_BLOCKS = 16
RNN_BLOCK = D_RNN // N_RNN_BLOCKS
CONV_WIDTH = 4
LRU_C = 8.0
D_FF = ((8 * D_MODEL + 3 * 256 - 1) // (3 * 256)) * 256
Q_BLOCK = 128
EPS = 1e-6

IN_WIDTHS = (D_RNN, D_RNN, D_ATTN, D_ATTN, D_ATTN, D_MODEL, D_MODEL)
D_IN = sum(IN_WIDTHS)
IN_SPLIT_POINTS = tuple(int(v) for v in np.cumsum(IN_WIDTHS)[:-1])

kernel_name = "hybrid_rglru_stickbreaking_block"


def rmsnorm(x, g):
    x32 = x.astype(jnp.float32)
    r = x32 * lax.rsqrt(jnp.mean(x32 * x32, axis=-1, keepdims=True) + EPS)
    return (r * g.astype(jnp.float32)).astype(x.dtype)


def causal_depthwise_conv(x, w, b):
    S = x.shape[1]
    xp = jnp.pad(x, ((0, 0), (CONV_WIDTH - 1, 0), (0, 0)))
    y = sum(xp[:, k:k + S, :] * w[k] for k in range(CONV_WIDTH))
    return y + b


def rg_lru(x, w_a, b_a, w_x, b_x, lam):
    B, S, _ = x.shape
    xb = x.reshape(B, S, N_RNN_BLOCKS, RNN_BLOCK)
    r = jax.nn.sigmoid(jnp.einsum('bsnc,ncd->bsnd', xb, w_a).reshape(B, S, D_RNN) + b_a)
    i = jax.nn.sigmoid(jnp.einsum('bsnc,ncd->bsnd', xb, w_x).reshape(B, S, D_RNN) + b_x)
    log_a = (-LRU_C * r.astype(jnp.float32) * jax.nn.softplus(-lam.astype(jnp.float32)))
    a = jnp.exp(log_a)
    u = jnp.sqrt(-jnp.expm1(2.0 * log_a)) * (i * x).astype(jnp.float32)

    def combine(c1, c2):
        a1, b1 = c1
        a2, b2 = c2
        return a1 * a2, a2 * b1 + b2

    _, h = lax.associative_scan(combine, (a, u), axis=1)
    return h.astype(x.dtype)


def head_rmsnorm(x, g):
    x32 = x.astype(jnp.float32)
    r = x32 * lax.rsqrt(jnp.mean(x32 * x32, axis=-1, keepdims=True) + EPS)
    return (r * g.astype(jnp.float32)).astype(x.dtype)


def stick_breaking_attention(q, k, v):
    S = q.shape[2]
    scale = 1.0 / math.sqrt(HEAD_DIM)
    outs = []
    for blk in range(S // Q_BLOCK):
        t0 = blk * Q_BLOCK
        tk = t0 + Q_BLOCK
        qb = q[:, :, t0:tk]
        kb = k[:, :, :tk]
        vb = v[:, :, :tk]
        z = jnp.einsum('bhqd,bhkd->bhqk', qb, kb).astype(jnp.float32) * scale
        q_pos = t0 + jnp.arange(Q_BLOCK)[:, None]
        k_pos = jnp.arange(tk)[None, :]
        causal = k_pos < q_pos
        log_keep = jnp.where(causal, jax.nn.log_sigmoid(-z), 0.0)
        incl = lax.cumsum(log_keep, axis=3, reverse=True)
        excl = jnp.concatenate([incl[..., 1:], jnp.zeros_like(incl[..., :1])], axis=-1)
        w = jnp.where(causal, jnp.exp(jax.nn.log_sigmoid(z) + excl), 0.0)
        outs.append(jnp.einsum('bhqk,bhkd->bhqd', w.astype(vb.dtype), vb))
    return jnp.concatenate(outs, axis=2)


def setup_inputs(seed: int = 0) -> dict:
    key = jax.random.key(seed)
    ks = jax.random.split(key, 24)
    f32 = jnp.float32

    def nrm(k, shape, fan_in):
        return jax.random.normal(k, shape, f32) * (fan_in ** -0.5)

    def gain(k, shape):
        return 1.0 + 0.02 * jax.random.normal(k, shape, f32)

    u = jax.random.uniform(ks[9], (DEPTH, D_RNN), f32, 0.9, 0.999)
    s = u ** (1.0 / LRU_C)
    lru_lambda = jnp.log(s) - jnp.log1p(-s)

    return {
        "x": jax.random.normal(ks[0], (BATCH, SEQ, D_MODEL), f32),
        "p": jax.random.normal(ks[1], (DEPTH, BATCH, SEQ, PLE_DIM), f32),
        "g_mix": gain(ks[2], (DEPTH, D_MODEL)),
        "w_in": nrm(ks[3], (DEPTH, D_MODEL, D_IN), D_MODEL),
        "conv_w": nrm(ks[4], (DEPTH, CONV_WIDTH, D_RNN), CONV_WIDTH),
        "conv_b": 0.02 * jax.random.normal(ks[5], (DEPTH, D_RNN), f32),
        "w_rg_a": nrm(ks[6], (DEPTH, N_RNN_BLOCKS, RNN_BLOCK, RNN_BLOCK), RNN_BLOCK),
        "b_rg_a": 0.02 * jax.random.normal(ks[7], (DEPTH, D_RNN), f32),
        "w_rg_x": nrm(ks[8], (DEPTH, N_RNN_BLOCKS, RNN_BLOCK, RNN_BLOCK), RNN_BLOCK),
        "b_rg_x": 0.02 * jax.random.normal(ks[10], (DEPTH, D_RNN), f32),
        "lru_lambda": lru_lambda,
        "q_gain": gain(ks[11], (DEPTH, HEAD_DIM)),
        "k_gain": gain(ks[12], (DEPTH, HEAD_DIM)),
        "w_rnn_out": nrm(ks[13], (DEPTH, D_RNN, D_MODEL), D_RNN),
        "w_attn_out": nrm(ks[14], (DEPTH, D_ATTN, D_MODEL), D_ATTN),
        "w_o": nrm(ks[15], (DEPTH, D_MODEL, D_MODEL), D_MODEL),
        "g_ffn": gain(ks[16], (DEPTH, D_MODEL)),
        "w_ffn_gu": nrm(ks[17], (DEPTH, D_MODEL, 2 * D_FF), D_MODEL),
        "w_ffn_down": nrm(ks[18], (DEPTH, D_FF, D_MODEL), D_FF),
        "g_ple": gain(ks[19], (DEPTH, D_MODEL)),
        "w_ple_gate": nrm(ks[20], (DEPTH, D_MODEL, D_MODEL), D_MODEL),
        "w_ple_proj": nrm(ks[21], (DEPTH, PLE_DIM, D_MODEL), PLE_DIM),
    }


def reference(x, p, g_mix, w_in, conv_w, conv_b, w_rg_a, b_rg_a, w_rg_x, b_rg_x,
              lru_lambda, q_gain, k_gain, w_rnn_out, w_attn_out, w_o, g_ffn,
              w_ffn_gu, w_ffn_down, g_ple, w_ple_gate, w_ple_proj):
    B, S, _ = x.shape
    for i in range(DEPTH):
        h = rmsnorm(x, g_mix[i])
        proj = h @ w_in[i]
        x_r, g_r, q, k, v, gate_r, gate_a = jnp.split(proj, IN_SPLIT_POINTS, axis=-1)

        xc = causal_depthwise_conv(x_r, conv_w[i], conv_b[i])
        y_r = jax.nn.gelu(g_r) * rg_lru(xc, w_rg_a[i], b_rg_a[i], w_rg_x[i], b_rg_x[i], lru_lambda[i])
        y_r = y_r @ w_rnn_out[i]

        q = head_rmsnorm(q.reshape(B, S, N_HEADS, HEAD_DIM).transpose(0, 2, 1, 3), q_gain[i])
        k = head_rmsnorm(k.reshape(B, S, N_HEADS, HEAD_DIM).transpose(0, 2, 1, 3), k_gain[i])
        v = v.reshape(B, S, N_HEADS, HEAD_DIM).transpose(0, 2, 1, 3)
        o = stick_breaking_attention(q, k, v)
        y_a = o.transpose(0, 2, 1, 3).reshape(B, S, D_ATTN) @ w_attn_out[i]

        mix = jax.nn.sigmoid(gate_r) * y_r + jax.nn.sigmoid(gate_a) * y_a
        x = x + mix @ w_o[i]

        h = rmsnorm(x, g_ffn[i])
        gu = h @ w_ffn_gu[i]
        g, u = jnp.split(gu, 2, axis=-1)
        x = x + (jax.nn.silu(g) * u) @ w_ffn_down[i]

        ple_gate = jax.nn.sigmoid(rmsnorm(x, g_ple[i]) @ w_ple_gate[i])
        x = x + ple_gate * (p[i] @ w_ple_proj[i])
    return x
```

```python
import functools
import math

import jax
import jax.numpy as jnp
import numpy as np
from jax import lax
from jax.experimental import pallas as pl
from jax.experimental.pallas import tpu as pltpu

F32 = jnp.float32
BF16 = jnp.bfloat16

EPS = 1e-6
LRU_C = 8.0
N_HEADS = 16
HEAD_DIM = 128
N_RNN_BLOCKS = 16
CONV_WIDTH = 4

LANES = 128
SUBLANES = 8
GATE_WINDOW = 4 * LANES
VMEM_LIMIT = 56 * 1024 * 1024


def _cparams(*sem):
    return pltpu.CompilerParams(dimension_semantics=sem, vmem_limit_bytes=VMEM_LIMIT)


def _tile(n, pref):
    t = min(n, pref)
    assert n % t == 0, (n, t)
    return t


def _rmsnorm_kernel(x_ref, g_ref, o_ref):
    x = x_ref[...]
    ms = jnp.mean(x * x, axis=-1, keepdims=True)
    o_ref[...] = (x * lax.rsqrt(ms + EPS) * g_ref[...]).astype(o_ref.dtype)


def _rmsnorm(x, g, tm=512):
    t, d = x.shape
    tm = _tile(t, tm)
    return pl.pallas_call(
        _rmsnorm_kernel,
        grid=(t // tm,),
        in_specs=[pl.BlockSpec((tm, d), lambda i: (i, 0)),
                  pl.BlockSpec((1, d), lambda i: (0, 0))],
        out_specs=pl.BlockSpec((tm, d), lambda i: (i, 0)),
        out_shape=jax.ShapeDtypeStruct((t, d), BF16),
        compiler_params=_cparams("parallel"),
        name="rmsnorm",
    )(x, g.reshape(1, d))


def _head_norm(acc, gain):
    outs = []
    for c in range(acc.shape[1] // HEAD_DIM):
        blk = acc[:, c * HEAD_DIM:(c + 1) * HEAD_DIM]
        ms = jnp.mean(blk * blk, axis=-1, keepdims=True)
        outs.append(blk * lax.rsqrt(ms + EPS) * gain[:, c * HEAD_DIM:(c + 1) * HEAD_DIM])
    return jnp.concatenate(outs, axis=1)


def _proj_kernel(h_ref, w_ref, *rest, kind):
    o_ref = rest[-1]
    acc = jnp.dot(h_ref[...], w_ref[...], preferred_element_type=F32)
    if kind == "gelu":
        acc = jax.nn.gelu(acc)
    elif kind == "sigmoid":
        acc = jax.nn.sigmoid(acc)
    elif kind == "headnorm":
        acc = _head_norm(acc, rest[0][...])
    o_ref[...] = acc.astype(o_ref.dtype)


def _proj(h, w, kind, out_dtype, gain=None, tm=1024, tn=1024):
    t, d = h.shape
    n = w.shape[1]
    tm = _tile(t, tm)
    tn = _tile(n, tn)
    in_specs = [pl.BlockSpec((tm, d), lambda i, j: (i, 0)),
                pl.BlockSpec((d, tn), lambda i, j: (0, j))]
    args = [h, w]
    if gain is not None:
        in_specs.append(pl.BlockSpec((1, tn), lambda i, j: (0, j)))
        args.append(gain)
    return pl.pallas_call(
        functools.partial(_proj_kernel, kind=kind),
        grid=(t // tm, n // tn),
        in_specs=in_specs,
        out_specs=pl.BlockSpec((tm, tn), lambda i, j: (i, j)),
        out_shape=jax.ShapeDtypeStruct((t, n), out_dtype),
        compiler_params=_cparams("parallel", "parallel"),
        name="proj_" + kind,
    )(*args)


def _gate_window_starts(d_rnn):
    blk = d_rnn // N_RNN_BLOCKS
    n_tiles = d_rnn // LANES
    starts = []
    for j in range(n_tiles):
        b0 = (LANES * j) // blk
        b1 = (LANES * j + LANES - 1) // blk
        s = min((blk * b0) // LANES, n_tiles - GATE_WINDOW // LANES)
        assert s * LANES <= blk * b0 and blk * (b1 + 1) <= s * LANES + GATE_WINDOW
        starts.append(s)
    return starts


def _band_gate_weights(w_a, w_x):
    nb, blk, _ = w_a.shape
    d_rnn = nb * blk
    eye = jnp.eye(nb, dtype=w_a.dtype)
    starts = _gate_window_starts(d_rnn)

    def band(w):
        dense = jnp.einsum('ncd,nm->ncmd', w, eye).reshape(d_rnn, d_rnn)
        return jnp.stack([dense[s * LANES:s * LANES + GATE_WINDOW, j * LANES:(j + 1) * LANES]
                          for j, s in enumerate(starts)])

    return jnp.concatenate([band(w_a), band(w_x)], axis=-1).astype(BF16)


def _rnn_kernel(xr_ref, gg_ref, cw_ref, cb_ref, wband_ref, ba_ref, bx_ref, lam_ref, y_ref,
                xext_ref, xc_ref, xcb_ref, hcar_ref, *, tc, d_rnn, starts):
    n_tiles = d_rnn // LANES

    @pl.when(pl.program_id(1) == 0)
    def _():
        xext_ref[0:SUBLANES, :] = jnp.zeros((SUBLANES, d_rnn), F32)
        hcar_ref[...] = jnp.zeros_like(hcar_ref)

    xext_ref[SUBLANES:SUBLANES + tc, :] = xr_ref[...]

    for j in range(n_tiles):
        cols = slice(j * LANES, (j + 1) * LANES)
        acc = jnp.broadcast_to(cb_ref[:, cols], (tc, LANES))
        for k in range(CONV_WIDTH):
            s = CONV_WIDTH - 1 - k
            acc = acc + cw_ref[k:k + 1, cols] * xext_ref[SUBLANES - s:SUBLANES - s + tc, cols]
        xc_ref[:, cols] = acc
        xcb_ref[:, cols] = acc.astype(BF16)

    xext_ref[0:SUBLANES, :] = xext_ref[tc:tc + SUBLANES, :]

    row_in_group = lax.broadcasted_iota(jnp.int32, (tc, LANES), 0) & (SUBLANES - 1)

    for j in range(n_tiles):
        cols = slice(j * LANES, (j + 1) * LANES)
        ks = starts[j] * LANES
        pre = jnp.dot(xcb_ref[:, ks:ks + GATE_WINDOW], wband_ref[j], preferred_element_type=F32)
        r = jax.nn.sigmoid(pre[:, :LANES] + ba_ref[:, cols])
        i = jax.nn.sigmoid(pre[:, LANES:] + bx_ref[:, cols])
        nlam = -lam_ref[:, cols]
        softplus = jnp.maximum(nlam, 0.0) + jnp.log1p(jnp.exp(-jnp.abs(nlam)))
        log_a = (-LRU_C) * r * softplus
        a = jnp.exp(log_a)
        u = jnp.sqrt(-jnp.tanh(log_a) * (a * a + 1.0)) * (i * xc_ref[:, cols])

        for d in (1, 2, 4):
            m = row_in_group >= d
            a_s = jnp.where(m, pltpu.roll(a, d, axis=0), 1.0)
            u_s = jnp.where(m, pltpu.roll(u, d, axis=0), 0.0)
            u = u + a * u_s
            a = a * a_s
        h = hcar_ref[:, cols]
        hs = []
        for g in range(tc // SUBLANES):
            rows = slice(g * SUBLANES, (g + 1) * SUBLANES)
            hg = u[rows] + a[rows] * h
            hs.append(hg)
            h = jnp.broadcast_to(hg[SUBLANES - 1:SUBLANES, :], (SUBLANES, LANES))
        hcar_ref[:, cols] = h
        y_ref[:, cols] = (gg_ref[:, cols] * jnp.concatenate(hs, axis=0)).astype(y_ref.dtype)


def _rnn(xr, gg, conv_w, conv_b, wband, b_a, b_x, lam, tc=256):
    b, s, d_rnn = xr.shape
    tc = _tile(s, tc)
    starts = _gate_window_starts(d_rnn)
    row = lambda v: v.reshape(1, d_rnn)
    full2 = lambda shape: pl.BlockSpec(shape, lambda bi, ci: (0, 0))
    chunk = pl.BlockSpec((None, tc, d_rnn), lambda bi, ci: (bi, ci, 0))
    return pl.pallas_call(
        functools.partial(_rnn_kernel, tc=tc, d_rnn=d_rnn, starts=starts),
        grid=(b, s // tc),
        in_specs=[chunk, chunk,
                  full2((CONV_WIDTH, d_rnn)), full2((1, d_rnn)),
                  pl.BlockSpec(wband.shape, lambda bi, ci: (0, 0, 0)),
                  full2((1, d_rnn)), full2((1, d_rnn)), full2((1, d_rnn))],
        out_specs=chunk,
        out_shape=jax.ShapeDtypeStruct((b, s, d_rnn), BF16),
        scratch_shapes=[pltpu.VMEM((tc + SUBLANES, d_rnn), F32),
                        pltpu.VMEM((tc, d_rnn), F32),
                        pltpu.VMEM((tc, d_rnn), BF16),
                        pltpu.VMEM((SUBLANES, d_rnn), F32)],
        compiler_params=_cparams("parallel", "arbitrary"),
        name="rnn",
    )(xr, gg, conv_w, row(conv_b), wband, row(b_a), row(b_x), row(lam))


def _attn_kernel(q_ref, k_ref, v_ref, tri_ref, o_ref, *, tq):
    qi = pl.program_id(2)
    q = q_ref[...]
    row = lax.broadcasted_iota(jnp.int32, (tq, tq), 0)
    col = lax.broadcasted_iota(jnp.int32, (tq, tq), 1)
    causal = col < row

    def block(kb, carry, diag):
        run, acc = carry
        start = pl.multiple_of(kb * tq, tq)
        k = k_ref[pl.ds(start, tq), :]
        v = v_ref[pl.ds(start, tq), :]
        z = lax.dot_general(q, k, (((1,), (1,)), ((), ())), preferred_element_type=F32)
        lk = -(jnp.maximum(z, 0.0) + jnp.log1p(jnp.exp(-jnp.abs(z))))
        if diag:
            lk = jnp.where(causal, lk, 0.0)
        hi = lk.astype(BF16)
        lo = (lk - hi.astype(F32)).astype(BF16)
        excl = (jnp.dot(hi, tri_ref[...], preferred_element_type=F32)
                + jnp.dot(lo, tri_ref[...], preferred_element_type=F32))
        w = jnp.exp(z + lk + excl + run)
        if diag:
            w = jnp.where(causal, w, 0.0)
        acc = acc + jnp.dot(w.astype(BF16), v, preferred_element_type=F32)
        run = run + jnp.sum(lk, axis=-1, keepdims=True)
        return run, acc

    carry = (jnp.zeros((tq, 1), F32), jnp.zeros((tq, HEAD_DIM), F32))
    carry = block(qi, carry, True)
    carry = lax.fori_loop(0, qi, lambda i, c: block(qi - 1 - i, c, False), carry)
    o_ref[...] = carry[1].astype(o_ref.dtype)


def _attention(qk, v, tq=256):
    b, s, _ = v.shape
    tq = _tile(s, tq)
    tri = jnp.asarray(np.tril(np.ones((tq, tq), np.float32), -1), BF16)
    return pl.pallas_call(
        functools.partial(_attn_kernel, tq=tq),
        grid=(b, N_HEADS, s // tq),
        in_specs=[pl.BlockSpec((None, tq, HEAD_DIM), lambda bi, h, qi: (bi, qi, h)),
                  pl.BlockSpec((None, s, HEAD_DIM), lambda bi, h, qi: (bi, 0, N_HEADS + h)),
                  pl.BlockSpec((None, s, HEAD_DIM), lambda bi, h, qi: (bi, 0, h)),
                  pl.BlockSpec((tq, tq), lambda bi, h, qi: (0, 0))],
        out_specs=pl.BlockSpec((None, tq, HEAD_DIM), lambda bi, h, qi: (bi, qi, h)),
        out_shape=jax.ShapeDtypeStruct(v.shape, BF16),
        compiler_params=_cparams("parallel", "parallel", "arbitrary"),
        name="attention",
    )(qk, qk, v, tri)


def _mix_kernel(yr_ref, oa_ref, wr_ref, wa_ref, sr_ref, sa_ref, o_ref):
    ya = jnp.dot(yr_ref[...], wr_ref[...], preferred_element_type=F32)
    yb = jnp.dot(oa_ref[...], wa_ref[...], preferred_element_type=F32)
    o_ref[...] = (sr_ref[...] * ya + sa_ref[...] * yb).astype(o_ref.dtype)


def _mix(yr, oa, w_r, w_a, gates, tm=512, tn=1024):
    t, d_rnn = yr.shape
    d_attn = oa.shape[1]
    n = w_r.shape[1]
    tm = _tile(t, tm)
    tn = _tile(n, tn)
    nj = n // tn
    return pl.pallas_call(
        _mix_kernel,
        grid=(t // tm, nj),
        in_specs=[pl.BlockSpec((tm, d_rnn), lambda i, j: (i, 0)),
                  pl.BlockSpec((tm, d_attn), lambda i, j: (i, 0)),
                  pl.BlockSpec((d_rnn, tn), lambda i, j: (0, j)),
                  pl.BlockSpec((d_attn, tn), lambda i, j: (0, j)),
                  pl.BlockSpec((tm, tn), lambda i, j: (i, j)),
                  pl.BlockSpec((tm, tn), lambda i, j: (i, nj + j))],
        out_specs=pl.BlockSpec((tm, tn), lambda i, j: (i, j)),
        out_shape=jax.ShapeDtypeStruct((t, n), BF16),
        compiler_params=_cparams("parallel", "parallel"),
        name="mix",
    )(yr, oa, w_r, w_a, gates, gates)


def _oproj_kernel(m_ref, w_ref, x_ref, g_ref, x1_ref, h_ref):
    x1 = x_ref[...] + jnp.dot(m_ref[...], w_ref[...], preferred_element_type=F32)
    x1_ref[...] = x1
    ms = jnp.mean(x1 * x1, axis=-1, keepdims=True)
    h_ref[...] = (x1 * lax.rsqrt(ms + EPS) * g_ref[...]).astype(h_ref.dtype)


def _oproj(mix, w_o, x, g_next, tm=512):
    t, d = x.shape
    tm = _tile(t, tm)
    return pl.pallas_call(
        _oproj_kernel,
        grid=(t // tm,),
        in_specs=[pl.BlockSpec((tm, d), lambda i: (i, 0)),
                  pl.BlockSpec((d, d), lambda i: (0, 0)),
                  pl.BlockSpec((tm, d), lambda i: (i, 0)),
                  pl.BlockSpec((1, d), lambda i: (0, 0))],
        out_specs=[pl.BlockSpec((tm, d), lambda i: (i, 0)),
                   pl.BlockSpec((tm, d), lambda i: (i, 0))],
        out_shape=[jax.ShapeDtypeStruct((t, d), F32), jax.ShapeDtypeStruct((t, d), BF16)],
        compiler_params=_cparams("parallel"),
        name="oproj",
    )(mix, w_o, x, g_next.reshape(1, d))


def _ffn_kernel(h_ref, wg_ref, wu_ref, wd_ref, x_ref, g_ref, x2_ref, h3_ref, acc_ref):
    f = pl.program_id(1)

    @pl.when(f == 0)
    def _():
        acc_ref[...] = x_ref[...]

    h = h_ref[...]
    g = jnp.dot(h, wg_ref[...], preferred_element_type=F32)
    u = jnp.dot(h, wu_ref[...], preferred_element_type=F32)
    act = (jax.nn.silu(g) * u).astype(BF16)
    acc_ref[...] += jnp.dot(act, wd_ref[...], preferred_element_type=F32)

    @pl.when(f == pl.num_programs(1) - 1)
    def _():
        x2 = acc_ref[...]
        x2_ref[...] = x2
        ms = jnp.mean(x2 * x2, axis=-1, keepdims=True)
        h3_ref[...] = (x2 * lax.rsqrt(ms + EPS) * g_ref[...]).astype(h3_ref.dtype)


def _ffn(h2, w_g, w_u, w_d, x1, g_next, tm=512, tf=512):
    t, d = x1.shape
    d_ff = w_g.shape[1]
    tm = _tile(t, tm)
    tf = _tile(d_ff, tf)
    return pl.pallas_call(
        _ffn_kernel,
        grid=(t // tm, d_ff // tf),
        in_specs=[pl.BlockSpec((tm, d), lambda i, f: (i, 0)),
                  pl.BlockSpec((d, tf), lambda i, f: (0, f)),
                  pl.BlockSpec((d, tf), lambda i, f: (0, f)),
                  pl.BlockSpec((tf, d), lambda i, f: (f, 0)),
                  pl.BlockSpec((tm, d), lambda i, f: (i, 0)),
                  pl.BlockSpec((1, d), lambda i, f: (0, 0))],
        out_specs=[pl.BlockSpec((tm, d), lambda i, f: (i, 0)),
                   pl.BlockSpec((tm, d), lambda i, f: (i, 0))],
        out_shape=[jax.ShapeDtypeStruct((t, d), F32), jax.ShapeDtypeStruct((t, d), BF16)],
        scratch_shapes=[pltpu.VMEM((tm, d), F32)],
        compiler_params=_cparams("parallel", "arbitrary"),
        name="ffn",
    )(h2, w_g, w_u, w_d, x1, g_next.reshape(1, d))


def _ple_kernel(h_ref, wg_ref, p_ref, wp_ref, x_ref, o_ref):
    gate = jax.nn.sigmoid(jnp.dot(h_ref[...], wg_ref[...], preferred_element_type=F32))
    emb = jnp.dot(p_ref[...], wp_ref[...], preferred_element_type=F32)
    o_ref[...] = x_ref[...] + gate * emb


def _ple(h3, w_gate, p, w_proj, x2, tm=1024, tn=1024):
    t, d = x2.shape
    dp = p.shape[1]
    tm = _tile(t, tm)
    tn = _tile(d, tn)
    return pl.pallas_call(
        _ple_kernel,
        grid=(t // tm, d // tn),
        in_specs=[pl.BlockSpec((tm, d), lambda i, j: (i, 0)),
                  pl.BlockSpec((d, tn), lambda i, j: (0, j)),
                  pl.BlockSpec((tm, dp), lambda i, j: (i, 0)),
                  pl.BlockSpec((dp, tn), lambda i, j: (0, j)),
                  pl.BlockSpec((tm, tn), lambda i, j: (i, j))],
        out_specs=pl.BlockSpec((tm, tn), lambda i, j: (i, j)),
        out_shape=jax.ShapeDtypeStruct((t, d), F32),
        compiler_params=_cparams("parallel", "parallel"),
        name="ple",
    )(h3, w_gate, p, w_proj, x2)


def _layer(x, p, g_mix, w_in, conv_w, conv_b, w_rg_a, b_rg_a, w_rg_x, b_rg_x, lru_lambda, q_gain,
           k_gain, w_rnn_out, w_attn_out, w_o, g_ffn, w_ffn_gu, w_ffn_down, g_ple, w_ple_gate,
           w_ple_proj):
    b, s, d = x.shape
    t = b * s
    d_rnn = w_rnn_out.shape[0]
    d_attn = w_attn_out.shape[0]
    d_ff = w_ffn_down.shape[0]
    x2d = x.reshape(t, d)

    o0 = 0
    o1 = o0 + d_rnn
    o2 = o1 + d_rnn
    o3 = o2 + 2 * d_attn
    o4 = o3 + d_attn
    o5 = o4 + 2 * d
    assert o5 == w_in.shape[1]
    wb = lambda w: w.astype(BF16)

    h = _rmsnorm(x2d, g_mix)
    xr = _proj(h, wb(w_in[:, o0:o1]), "none", F32, tn=d_rnn // 2)
    gg = _proj(h, wb(w_in[:, o1:o2]), "gelu", F32, tn=d_rnn // 2)
    qk_gain = jnp.concatenate([jnp.tile(q_gain * (1.0 / math.sqrt(HEAD_DIM)), N_HEADS),
                               jnp.tile(k_gain, N_HEADS)]).reshape(1, 2 * d_attn)
    qk = _proj(h, wb(w_in[:, o2:o3]), "headnorm", BF16, gain=qk_gain)
    v = _proj(h, wb(w_in[:, o3:o4]), "none", BF16)
    gates = _proj(h, wb(w_in[:, o4:o5]), "sigmoid", F32)

    wband = _band_gate_weights(w_rg_a, w_rg_x)
    yr = _rnn(xr.reshape(b, s, d_rnn), gg.reshape(b, s, d_rnn), conv_w, conv_b, wband,
              b_rg_a, b_rg_x, lru_lambda)
    oa = _attention(qk.reshape(b, s, 2 * d_attn), v.reshape(b, s, d_attn))

    mix = _mix(yr.reshape(t, d_rnn), oa.reshape(t, d_attn), wb(w_rnn_out), wb(w_attn_out), gates)
    x1, h2 = _oproj(mix, wb(w_o), x2d, g_ffn)
    x2, h3 = _ffn(h2, wb(w_ffn_gu[:, :d_ff]), wb(w_ffn_gu[:, d_ff:]), wb(w_ffn_down), x1, g_ple)
    out = _ple(h3, wb(w_ple_gate), wb(p.reshape(t, -1)), wb(w_ple_proj), x2)
    return out.reshape(b, s, d)


def kernel(x, p, g_mix, w_in, conv_w, conv_b, w_rg_a, b_rg_a, w_rg_x, b_rg_x, lru_lambda, q_gain,
           k_gain, w_rnn_out, w_attn_out, w_o, g_ffn, w_ffn_gu, w_ffn_down, g_ple, w_ple_gate,
           w_ple_proj):
    params = (g_mix, w_in, conv_w, conv_b, w_rg_a, b_rg_a, w_rg_x, b_rg_x, lru_lambda, q_gain,
              k_gain, w_rnn_out, w_attn_out, w_o, g_ffn, w_ffn_gu, w_ffn_down, g_ple, w_ple_gate,
              w_ple_proj)
    for i in range(p.shape[0]):
        x = _layer(x, p[i], *[a[i] for a in params])
    return x
```

```python
import functools
import math

import jax
import jax.numpy as jnp
import numpy as np
from jax import lax
from jax.experimental import pallas as pl
from jax.experimental.pallas import tpu as pltpu

F32 = jnp.float32
BF16 = jnp.bfloat16

EPS = 1e-6
LRU_C = 8.0
N_HEADS = 16
HEAD_DIM = 128
N_RNN_BLOCKS = 16
CONV_WIDTH = 4

LANES = 128
SUBLANES = 8
GATE_WINDOW = 4 * LANES
VMEM_LIMIT = 56 * 1024 * 1024
F32_TINY = float(np.finfo(np.float32).tiny)


def _cparams(*sem):
    return pltpu.CompilerParams(dimension_semantics=sem, vmem_limit_bytes=VMEM_LIMIT)


def _tile(n, pref):
    t = min(n, pref)
    assert n % t == 0, (n, t)
    return t


def _rmsnorm_kernel(x_ref, g_ref, o_ref):
    x = x_ref[...]
    ms = jnp.mean(x * x, axis=-1, keepdims=True)
    o_ref[...] = (x * lax.rsqrt(ms + EPS) * g_ref[...]).astype(o_ref.dtype)


def _rmsnorm(x, g, tm=512):
    t, d = x.shape
    tm = _tile(t, tm)
    return pl.pallas_call(
        _rmsnorm_kernel,
        grid=(t // tm,),
        in_specs=[pl.BlockSpec((tm, d), lambda i: (i, 0)),
                  pl.BlockSpec((1, d), lambda i: (0, 0))],
        out_specs=pl.BlockSpec((tm, d), lambda i: (i, 0)),
        out_shape=jax.ShapeDtypeStruct((t, d), BF16),
        compiler_params=_cparams("parallel"),
        name="rmsnorm",
    )(x, g.reshape(1, d))


def _head_norm(acc, gain):
    outs = []
    for c in range(acc.shape[1] // HEAD_DIM):
        blk = acc[:, c * HEAD_DIM:(c + 1) * HEAD_DIM]
        ms = jnp.mean(blk * blk, axis=-1, keepdims=True)
        outs.append(blk * lax.rsqrt(ms + EPS) * gain[:, c * HEAD_DIM:(c + 1) * HEAD_DIM])
    return jnp.concatenate(outs, axis=1)


def _proj_kernel(h_ref, w_ref, *rest, kind):
    *extra, o_ref, wb_ref = rest

    @pl.when(pl.program_id(1) == 0)
    def _():
        wb_ref[...] = w_ref[...].astype(BF16)

    acc = jnp.dot(h_ref[...], wb_ref[...], preferred_element_type=F32)
    if kind == "gelu":
        acc = jax.nn.gelu(acc)
    elif kind == "sigmoid":
        acc = jax.nn.sigmoid(acc)
    elif kind == "headnorm":
        acc = _head_norm(acc, extra[0][...])
    o_ref[...] = acc.astype(o_ref.dtype)


def _proj(h, w, col0, n, kind, out_dtype, gain=None, tm=1024, tn=1024):
    t, d = h.shape
    tm = _tile(t, tm)
    tn = _tile(n, tn)
    assert col0 % LANES == 0 and tn % LANES == 0
    in_specs = [pl.BlockSpec((tm, d), lambda j, i: (i, 0)),
                pl.BlockSpec((pl.Element(d), pl.Element(tn)),
                             lambda j, i: (0, pl.multiple_of(col0 + j * tn, LANES)))]
    args = [h, w]
    if gain is not None:
        in_specs.append(pl.BlockSpec((1, tn), lambda j, i: (0, j)))
        args.append(gain)
    return pl.pallas_call(
        functools.partial(_proj_kernel, kind=kind),
        grid=(n // tn, t // tm),
        in_specs=in_specs,
        out_specs=pl.BlockSpec((tm, tn), lambda j, i: (i, j)),
        out_shape=jax.ShapeDtypeStruct((t, n), out_dtype),
        scratch_shapes=[pltpu.VMEM((d, tn), BF16)],
        compiler_params=_cparams("parallel", "arbitrary"),
        name="proj_" + kind,
    )(*args)


def _gate_window_starts(d_rnn):
    blk = d_rnn // N_RNN_BLOCKS
    n_tiles = d_rnn // LANES
    starts = []
    for j in range(n_tiles):
        b0 = (LANES * j) // blk
        b1 = (LANES * j + LANES - 1) // blk
        s = min((blk * b0) // LANES, n_tiles - GATE_WINDOW // LANES)
        assert s * LANES <= blk * b0 and blk * (b1 + 1) <= s * LANES + GATE_WINDOW
        starts.append(s)
    return starts


def _band_gate_weights(w_a, w_x):
    nb, blk, _ = w_a.shape
    d_rnn = nb * blk
    starts = _gate_window_starts(d_rnn)

    def band(w):
        tiles = []
        for j, s in enumerate(starts):
            c_lo, c_hi = LANES * j, LANES * (j + 1)
            pieces = []
            for b in range(c_lo // blk, (c_hi - 1) // blk + 1):
                oc0, oc1 = max(c_lo, blk * b), min(c_hi, blk * (b + 1))
                r0 = blk * b - s * LANES
                pieces.append(jnp.pad(w[b, :, oc0 - blk * b:oc1 - blk * b],
                                      ((r0, GATE_WINDOW - blk - r0), (0, 0))))
            tiles.append(jnp.concatenate(pieces, axis=1))
        return jnp.stack(tiles)

    return jnp.concatenate([band(w_a), band(w_x)], axis=-1).astype(BF16)


def _rnn_kernel(xr_ref, gg_ref, cw_ref, cb_ref, wband_ref, ba_ref, bx_ref, lam_ref, y_ref,
                tail_ref, xc_ref, xcb_ref, hcar_ref, *, tc, d_rnn, starts):
    n_tiles = d_rnn // LANES

    @pl.when(pl.program_id(1) == 0)
    def _():
        tail_ref[...] = jnp.zeros_like(tail_ref)
        hcar_ref[...] = jnp.zeros_like(hcar_ref)

    n_groups = tc // SUBLANES
    row_in_group = lax.broadcasted_iota(jnp.int32, (n_groups, SUBLANES, LANES), 1)

    for j in range(n_tiles):
        cols = slice(j * LANES, (j + 1) * LANES)
        x = xr_ref[:, cols].reshape(n_groups, SUBLANES, LANES)
        tail = tail_ref[:, cols].reshape(1, SUBLANES, LANES)
        acc = cb_ref[:, cols] + cw_ref[CONV_WIDTH - 1:CONV_WIDTH, cols] * x
        for s in range(1, CONV_WIDTH):
            rolled = pltpu.roll(x, s, axis=1)
            wrapped = jnp.concatenate([pltpu.roll(tail, s, axis=1), rolled[:-1]], axis=0)
            k = CONV_WIDTH - 1 - s
            acc = acc + cw_ref[k:k + 1, cols] * jnp.where(row_in_group >= s, rolled, wrapped)
        acc = acc.reshape(tc, LANES)
        xc_ref[:, cols] = acc
        xcb_ref[:, cols] = acc.astype(BF16)

    tail_ref[...] = xr_ref[tc - SUBLANES:tc, :]

    for j in range(n_tiles):
        cols = slice(j * LANES, (j + 1) * LANES)
        ks = starts[j] * LANES
        pre = jnp.dot(xcb_ref[:, ks:ks + GATE_WINDOW], wband_ref[j], preferred_element_type=F32)
        r = jax.nn.sigmoid(pre[:, :LANES] + ba_ref[:, cols])
        i = jax.nn.sigmoid(pre[:, LANES:] + bx_ref[:, cols])
        nlam = -lam_ref[:, cols]
        softplus = jnp.maximum(nlam, 0.0) + jnp.log1p(jnp.exp(-jnp.abs(nlam)))
        log_a = (-LRU_C) * r * softplus
        a = jnp.exp(log_a)
        y = -jnp.tanh(log_a) * (a * a + 1.0)
        u = y * lax.rsqrt(jnp.maximum(y, F32_TINY)) * (i * xc_ref[:, cols])

        a = a.reshape(n_groups, SUBLANES, LANES)
        u = u.reshape(n_groups, SUBLANES, LANES)
        for d in (1, 2, 4):
            m = row_in_group >= d
            a_s = jnp.where(m, pltpu.roll(a, d, axis=1), 1.0)
            u_s = jnp.where(m, pltpu.roll(u, d, axis=1), 0.0)
            u = u + a * u_s
            a = a * a_s
        h = hcar_ref[:, cols]
        hs = []
        for g in range(n_groups):
            hg = u[g] + a[g] * h
            hs.append(hg)
            h = jnp.broadcast_to(hg[SUBLANES - 1:SUBLANES, :], (SUBLANES, LANES))
        hcar_ref[:, cols] = h
        y_ref[:, cols] = (gg_ref[:, cols] * jnp.concatenate(hs, axis=0)).astype(y_ref.dtype)


def _rnn(xr, gg, conv_w, conv_b, wband, b_a, b_x, lam, tc=256):
    b, s, d_rnn = xr.shape
    tc = _tile(s, tc)
    starts = _gate_window_starts(d_rnn)
    row = lambda v: v.reshape(1, d_rnn)
    full2 = lambda shape: pl.BlockSpec(shape, lambda bi, ci: (0, 0))
    chunk = pl.BlockSpec((None, tc, d_rnn), lambda bi, ci: (bi, ci, 0))
    return pl.pallas_call(
        functools.partial(_rnn_kernel, tc=tc, d_rnn=d_rnn, starts=starts),
        grid=(b, s // tc),
        in_specs=[chunk, chunk,
                  full2((CONV_WIDTH, d_rnn)), full2((1, d_rnn)),
                  pl.BlockSpec(wband.shape, lambda bi, ci: (0, 0, 0)),
                  full2((1, d_rnn)), full2((1, d_rnn)), full2((1, d_rnn))],
        out_specs=chunk,
        out_shape=jax.ShapeDtypeStruct((b, s, d_rnn), BF16),
        scratch_shapes=[pltpu.VMEM((SUBLANES, d_rnn), F32),
                        pltpu.VMEM((tc, d_rnn), F32),
                        pltpu.VMEM((tc, d_rnn), BF16),
                        pltpu.VMEM((SUBLANES, d_rnn), F32)],
        compiler_params=_cparams("parallel", "arbitrary"),
        name="rnn",
    )(xr, gg, conv_w, row(conv_b), wband, row(b_a), row(b_x), row(lam))


PRUNE_LOG = -104.0


def _attn_kernel(q_ref, k_ref, v_ref, tri_ref, o_ref, *, tq, hp):
    qi = pl.program_id(2)
    row = lax.broadcasted_iota(jnp.int32, (tq, tq), 0)
    col = lax.broadcasted_iota(jnp.int32, (tq, tq), 1)
    causal = col < row

    def block(hd, kb, run, acc, diag):
        lanes = slice(hd * HEAD_DIM, (hd + 1) * HEAD_DIM)
        start = pl.multiple_of(kb * tq, tq)
        k = k_ref[pl.ds(start, tq), lanes]
        v = v_ref[pl.ds(start, tq), lanes]
        z = lax.dot_general(q_ref[:, lanes], k, (((1,), (1,)), ((), ())), preferred_element_type=F32)
        neg_abs = lax.bitcast_convert_type(lax.bitcast_convert_type(z, jnp.uint32) | jnp.uint32(0x80000000), F32)
        lk = -(jnp.maximum(z, 0.0) + jnp.log(1.0 + jnp.exp(neg_abs)))
        if diag:
            lk = jnp.where(causal, lk, 0.0)
        excl = jnp.dot(lk.astype(BF16), tri_ref[...], preferred_element_type=F32)
        w = jnp.exp(z + lk + excl + run)
        if diag:
            w = jnp.where(causal, w, 0.0)
        acc = acc + jnp.dot(w.astype(BF16), v, preferred_element_type=F32)
        run = run + jnp.sum(lk, axis=-1, keepdims=True)
        return run, acc

    def run_max(state):
        m = jnp.max(state[0][0])
        for run, _ in state[1:]:
            m = jnp.maximum(m, jnp.max(run))
        return m

    zero = (jnp.zeros((tq, 1), F32), jnp.zeros((tq, HEAD_DIM), F32))
    state = tuple(block(hd, qi, *zero, True) for hd in range(hp))

    def cond(c):
        kb, rmax, _ = c
        return jnp.logical_and(kb >= 0, rmax > PRUNE_LOG)

    def body(c):
        kb, _, st = c
        st = tuple(block(hd, kb, run, acc, False) for hd, (run, acc) in enumerate(st))
        return kb - 1, run_max(st), st

    _, _, state = lax.while_loop(cond, body, (qi - 1, run_max(state), state))
    for hd, (_, acc) in enumerate(state):
        o_ref[:, hd * HEAD_DIM:(hd + 1) * HEAD_DIM] = acc.astype(o_ref.dtype)


def _attention(qk, v, tq=256, hp=4):
    b, s, _ = v.shape
    tq = _tile(s, tq)
    hw = hp * HEAD_DIM
    n_hg = N_HEADS // hp
    tri = jnp.asarray(np.tril(np.ones((tq, tq), np.float32), -1), BF16)
    return pl.pallas_call(
        functools.partial(_attn_kernel, tq=tq, hp=hp),
        grid=(b, n_hg, s // tq),
        in_specs=[pl.BlockSpec((None, tq, hw), lambda bi, h, qi: (bi, qi, h)),
                  pl.BlockSpec((None, s, hw), lambda bi, h, qi: (bi, 0, n_hg + h)),
                  pl.BlockSpec((None, s, hw), lambda bi, h, qi: (bi, 0, h)),
                  pl.BlockSpec((tq, tq), lambda bi, h, qi: (0, 0))],
        out_specs=pl.BlockSpec((None, tq, hw), lambda bi, h, qi: (bi, qi, h)),
        out_shape=jax.ShapeDtypeStruct(v.shape, BF16),
        compiler_params=_cparams("parallel", "parallel", "arbitrary"),
        name="attention",
    )(qk, qk, v, tri)


def _mix_kernel(yr_ref, oa_ref, wr_ref, wa_ref, sr_ref, sa_ref, o_ref):
    ya = jnp.dot(yr_ref[...], wr_ref[...], preferred_element_type=F32)
    yb = jnp.dot(oa_ref[...], wa_ref[...], preferred_element_type=F32)
    o_ref[...] = (sr_ref[...] * ya + sa_ref[...] * yb).astype(o_ref.dtype)


def _mix(yr, oa, w_r, w_a, gates, tm=512, tn=1024):
    t, d_rnn = yr.shape
    d_attn = oa.shape[1]
    n = w_r.shape[1]
    tm = _tile(t, tm)
    tn = _tile(n, tn)
    nj = n // tn
    return pl.pallas_call(
        _mix_kernel,
        grid=(t // tm, nj),
        in_specs=[pl.BlockSpec((tm, d_rnn), lambda i, j: (i, 0)),
                  pl.BlockSpec((tm, d_attn), lambda i, j: (i, 0)),
                  pl.BlockSpec((d_rnn, tn), lambda i, j: (0, j)),
                  pl.BlockSpec((d_attn, tn), lambda i, j: (0, j)),
                  pl.BlockSpec((tm, tn), lambda i, j: (i, j)),
                  pl.BlockSpec((tm, tn), lambda i, j: (i, nj + j))],
        out_specs=pl.BlockSpec((tm, tn), lambda i, j: (i, j)),
        out_shape=jax.ShapeDtypeStruct((t, n), BF16),
        compiler_params=_cparams("parallel", "parallel"),
        name="mix",
    )(yr, oa, w_r, w_a, gates, gates)


def _oproj_kernel(m_ref, w_ref, x_ref, g_ref, x1_ref, h_ref):
    x1 = x_ref[...] + jnp.dot(m_ref[...], w_ref[...], preferred_element_type=F32)
    x1_ref[...] = x1
    ms = jnp.mean(x1 * x1, axis=-1, keepdims=True)
    h_ref[...] = (x1 * lax.rsqrt(ms + EPS) * g_ref[...]).astype(h_ref.dtype)


def _oproj(mix, w_o, x, g_next, tm=512):
    t, d = x.shape
    tm = _tile(t, tm)
    return pl.pallas_call(
        _oproj_kernel,
        grid=(t // tm,),
        in_specs=[pl.BlockSpec((tm, d), lambda i: (i, 0)),
                  pl.BlockSpec((d, d), lambda i: (0, 0)),
                  pl.BlockSpec((tm, d), lambda i: (i, 0)),
                  pl.BlockSpec((1, d), lambda i: (0, 0))],
        out_specs=[pl.BlockSpec((tm, d), lambda i: (i, 0)),
                   pl.BlockSpec((tm, d), lambda i: (i, 0))],
        out_shape=[jax.ShapeDtypeStruct((t, d), F32), jax.ShapeDtypeStruct((t, d), BF16)],
        compiler_params=_cparams("parallel"),
        name="oproj",
    )(mix, w_o, x, g_next.reshape(1, d))


def _ffn_kernel(h_ref, wg_ref, wu_ref, wd_ref, x_ref, g_ref, x2_ref, h3_ref, acc_ref):
    f = pl.program_id(1)

    @pl.when(f == 0)
    def _():
        acc_ref[...] = x_ref[...]

    h = h_ref[...]
    g = jnp.dot(h, wg_ref[...], preferred_element_type=F32)
    u = jnp.dot(h, wu_ref[...], preferred_element_type=F32)
    act = (jax.nn.silu(g) * u).astype(BF16)
    acc_ref[...] += jnp.dot(act, wd_ref[...], preferred_element_type=F32)

    @pl.when(f == pl.num_programs(1) - 1)
    def _():
        x2 = acc_ref[...]
        x2_ref[...] = x2
        ms = jnp.mean(x2 * x2, axis=-1, keepdims=True)
        h3_ref[...] = (x2 * lax.rsqrt(ms + EPS) * g_ref[...]).astype(h3_ref.dtype)


def _ffn(h2, w_g, w_u, w_d, x1, g_next, tm=512, tf=512):
    t, d = x1.shape
    d_ff = w_g.shape[1]
    tm = _tile(t, tm)
    tf = _tile(d_ff, tf)
    return pl.pallas_call(
        _ffn_kernel,
        grid=(t // tm, d_ff // tf),
        in_specs=[pl.BlockSpec((tm, d), lambda i, f: (i, 0)),
                  pl.BlockSpec((d, tf), lambda i, f: (0, f)),
                  pl.BlockSpec((d, tf), lambda i, f: (0, f)),
                  pl.BlockSpec((tf, d), lambda i, f: (f, 0)),
                  pl.BlockSpec((tm, d), lambda i, f: (i, 0)),
                  pl.BlockSpec((1, d), lambda i, f: (0, 0))],
        out_specs=[pl.BlockSpec((tm, d), lambda i, f: (i, 0)),
                   pl.BlockSpec((tm, d), lambda i, f: (i, 0))],
        out_shape=[jax.ShapeDtypeStruct((t, d), F32), jax.ShapeDtypeStruct((t, d), BF16)],
        scratch_shapes=[pltpu.VMEM((tm, d), F32)],
        compiler_params=_cparams("parallel", "arbitrary"),
        name="ffn",
    )(h2, w_g, w_u, w_d, x1, g_next.reshape(1, d))


def _ple_kernel(h_ref, wg_ref, p_ref, wp_ref, x_ref, o_ref):
    gate = jax.nn.sigmoid(jnp.dot(h_ref[...], wg_ref[...], preferred_element_type=F32))
    emb = jnp.dot(p_ref[...], wp_ref[...], preferred_element_type=F32)
    o_ref[...] = x_ref[...] + gate * emb


def _ple(h3, w_gate, p, w_proj, x2, tm=1024, tn=1024):
    t, d = x2.shape
    dp = p.shape[1]
    tm = _tile(t, tm)
    tn = _tile(d, tn)
    return pl.pallas_call(
        _ple_kernel,
        grid=(t // tm, d // tn),
        in_specs=[pl.BlockSpec((tm, d), lambda i, j: (i, 0)),
                  pl.BlockSpec((d, tn), lambda i, j: (0, j)),
                  pl.BlockSpec((tm, dp), lambda i, j: (i, 0)),
                  pl.BlockSpec((dp, tn), lambda i, j: (0, j)),
                  pl.BlockSpec((tm, tn), lambda i, j: (i, j))],
        out_specs=pl.BlockSpec((tm, tn), lambda i, j: (i, j)),
        out_shape=jax.ShapeDtypeStruct((t, d), F32),
        compiler_params=_cparams("parallel", "parallel"),
        name="ple",
    )(h3, w_gate, p, w_proj, x2)


def _layer(x, p, g_mix, w_in, conv_w, conv_b, w_rg_a, b_rg_a, w_rg_x, b_rg_x, lru_lambda, q_gain,
           k_gain, w_rnn_out, w_attn_out, w_o, g_ffn, w_ffn_gu, w_ffn_down, g_ple, w_ple_gate,
           w_ple_proj):
    b, s, d = x.shape
    t = b * s
    d_rnn = w_rnn_out.shape[0]
    d_attn = w_attn_out.shape[0]
    d_ff = w_ffn_down.shape[0]
    x2d = x.reshape(t, d)

    o0 = 0
    o1 = o0 + d_rnn
    o2 = o1 + d_rnn
    o3 = o2 + 2 * d_attn
    o4 = o3 + d_attn
    o5 = o4 + 2 * d
    assert o5 == w_in.shape[1]
    wb = lambda w: w.astype(BF16)

    h = _rmsnorm(x2d, g_mix)
    xr = _proj(h, w_in, o0, d_rnn, "none", F32, tn=d_rnn // 2)
    gg = _proj(h, w_in, o1, d_rnn, "gelu", F32, tn=d_rnn // 2)
    qk_gain = jnp.concatenate([jnp.tile(q_gain * (1.0 / math.sqrt(HEAD_DIM)), N_HEADS),
                               jnp.tile(k_gain, N_HEADS)]).reshape(1, 2 * d_attn)
    qk = _proj(h, w_in, o2, 2 * d_attn, "headnorm", BF16, gain=qk_gain)
    v = _proj(h, w_in, o3, d_attn, "none", BF16)
    gates = _proj(h, w_in, o4, 2 * d, "sigmoid", F32)

    wband = _band_gate_weights(w_rg_a, w_rg_x)
    yr = _rnn(xr.reshape(b, s, d_rnn), gg.reshape(b, s, d_rnn), conv_w, conv_b, wband,
              b_rg_a, b_rg_x, lru_lambda)
    oa = _attention(qk.reshape(b, s, 2 * d_attn), v.reshape(b, s, d_attn))

    mix = _mix(yr.reshape(t, d_rnn), oa.reshape(t, d_attn), wb(w_rnn_out), wb(w_attn_out), gates)
    x1, h2 = _oproj(mix, wb(w_o), x2d, g_ffn)
    x2, h3 = _ffn(h2, wb(w_ffn_gu[:, :d_ff]), wb(w_ffn_gu[:, d_ff:]), wb(w_ffn_down), x1, g_ple)
    out = _ple(h3, wb(w_ple_gate), wb(p.reshape(t, -1)), wb(w_ple_proj), x2)
    return out.reshape(b, s, d)


def kernel(x, p, g_mix, w_in, conv_w, conv_b, w_rg_a, b_rg_a, w_rg_x, b_rg_x, lru_lambda, q_gain,
           k_gain, w_rnn_out, w_attn_out, w_o, g_ffn, w_ffn_gu, w_ffn_down, g_ple, w_ple_gate,
           w_ple_proj):
    params = (g_mix, w_in, conv_w, conv_b, w_rg_a, b_rg_a, w_rg_x, b_rg_x, lru_lambda, q_gain,
              k_gain, w_rnn_out, w_attn_out, w_o, g_ffn, w_ffn_gu, w_ffn_down, g_ple, w_ple_gate,
              w_ple_proj)
    for i in range(p.shape[0]):
        x = _layer(x, p[i], *[a[i] for a in params])
    return x
```

```python
import functools
import math

import jax
import jax.numpy as jnp
import numpy as np
from jax import lax
from jax.experimental import pallas as pl
from jax.experimental.pallas import tpu as pltpu

F32 = jnp.float32
BF16 = jnp.bfloat16

EPS = 1e-6
LRU_C = 8.0
N_HEADS = 16
HEAD_DIM = 128
N_RNN_BLOCKS = 16
CONV_WIDTH = 4

LANES = 128
SUBLANES = 8
GATE_WINDOW = 4 * LANES
VMEM_LIMIT = 56 * 1024 * 1024
F32_TINY = float(np.finfo(np.float32).tiny)


def _cparams(*sem):
    return pltpu.CompilerParams(dimension_semantics=sem, vmem_limit_bytes=VMEM_LIMIT)


def _tile(n, pref):
    t = min(n, pref)
    assert n % t == 0, (n, t)
    return t


def _rmsnorm_kernel(x_ref, g_ref, o_ref):
    x = x_ref[...]
    ms = jnp.mean(x * x, axis=-1, keepdims=True)
    o_ref[...] = (x * lax.rsqrt(ms + EPS) * g_ref[...]).astype(o_ref.dtype)


def _rmsnorm(x, g, tm=512):
    t, d = x.shape
    tm = _tile(t, tm)
    return pl.pallas_call(
        _rmsnorm_kernel,
        grid=(t // tm,),
        in_specs=[pl.BlockSpec((tm, d), lambda i: (i, 0)),
                  pl.BlockSpec((1, d), lambda i: (0, 0))],
        out_specs=pl.BlockSpec((tm, d), lambda i: (i, 0)),
        out_shape=jax.ShapeDtypeStruct((t, d), BF16),
        compiler_params=_cparams("parallel"),
        name="rmsnorm",
    )(x, g.reshape(1, d))


def _head_norm(acc, gain):
    outs = []
    for c in range(acc.shape[1] // HEAD_DIM):
        blk = acc[:, c * HEAD_DIM:(c + 1) * HEAD_DIM]
        ms = jnp.mean(blk * blk, axis=-1, keepdims=True)
        outs.append(blk * lax.rsqrt(ms + EPS) * gain[:, c * HEAD_DIM:(c + 1) * HEAD_DIM])
    return jnp.concatenate(outs, axis=1)


def _proj_kernel(h_ref, w_ref, *rest, kind):
    *extra, o_ref, wb_ref = rest

    @pl.when(pl.program_id(1) == 0)
    def _():
        wb_ref[...] = w_ref[...].astype(BF16)

    acc = jnp.dot(h_ref[...], wb_ref[...], preferred_element_type=F32)
    if kind == "gelu":
        acc = jax.nn.gelu(acc)
    elif kind == "sigmoid":
        acc = jax.nn.sigmoid(acc)
    elif kind == "headnorm":
        acc = _head_norm(acc, extra[0][...])
    o_ref[...] = acc.astype(o_ref.dtype)


def _proj(h, w, col0, n, kind, out_dtype, gain=None, tm=1024, tn=1024):
    t, d = h.shape
    tm = _tile(t, tm)
    tn = _tile(n, tn)
    assert col0 % LANES == 0 and tn % LANES == 0
    in_specs = [pl.BlockSpec((tm, d), lambda j, i: (i, 0)),
                pl.BlockSpec((pl.Element(d), pl.Element(tn)),
                             lambda j, i: (0, pl.multiple_of(col0 + j * tn, LANES)))]
    args = [h, w]
    if gain is not None:
        in_specs.append(pl.BlockSpec((1, tn), lambda j, i: (0, j)))
        args.append(gain)
    return pl.pallas_call(
        functools.partial(_proj_kernel, kind=kind),
        grid=(n // tn, t // tm),
        in_specs=in_specs,
        out_specs=pl.BlockSpec((tm, tn), lambda j, i: (i, j)),
        out_shape=jax.ShapeDtypeStruct((t, n), out_dtype),
        scratch_shapes=[pltpu.VMEM((d, tn), BF16)],
        compiler_params=_cparams("parallel", "arbitrary"),
        name="proj_" + kind,
    )(*args)


def _gate_window_starts(d_rnn):
    blk = d_rnn // N_RNN_BLOCKS
    n_tiles = d_rnn // LANES
    starts = []
    for j in range(n_tiles):
        b0 = (LANES * j) // blk
        b1 = (LANES * j + LANES - 1) // blk
        s = min((blk * b0) // LANES, n_tiles - GATE_WINDOW // LANES)
        assert s * LANES <= blk * b0 and blk * (b1 + 1) <= s * LANES + GATE_WINDOW
        starts.append(s)
    return starts


def _band_gate_weights(w_a, w_x):
    nb, blk, _ = w_a.shape
    d_rnn = nb * blk
    starts = _gate_window_starts(d_rnn)

    def band(w):
        tiles = []
        for j, s in enumerate(starts):
            c_lo, c_hi = LANES * j, LANES * (j + 1)
            pieces = []
            for b in range(c_lo // blk, (c_hi - 1) // blk + 1):
                oc0, oc1 = max(c_lo, blk * b), min(c_hi, blk * (b + 1))
                r0 = blk * b - s * LANES
                pieces.append(jnp.pad(w[b, :, oc0 - blk * b:oc1 - blk * b],
                                      ((r0, GATE_WINDOW - blk - r0), (0, 0))))
            tiles.append(jnp.concatenate(pieces, axis=1))
        return jnp.stack(tiles)

    return jnp.concatenate([band(w_a), band(w_x)], axis=-1).astype(BF16)


def _rnn_kernel(xr_ref, gg_ref, cw_ref, cb_ref, wband_ref, ba_ref, bx_ref, lam_ref, y_ref,
                tail_ref, xin_ref, hout_ref, xc_ref, xcb_ref, hcar_ref, *, tc, d_rnn, starts):
    n_tiles = d_rnn // LANES
    seg = tc // SUBLANES
    pitch = seg + SUBLANES

    @pl.when(pl.program_id(1) == 0)
    def _():
        tail_ref[...] = jnp.zeros_like(tail_ref)
        hcar_ref[...] = jnp.zeros_like(hcar_ref)

    sub = lax.broadcasted_iota(jnp.int32, (SUBLANES, LANES), 0)

    def seg_rows(p):
        return slice(p * seg, (p + 1) * seg)

    def slab_rows(p):
        return slice(p * pitch, p * pitch + seg)

    for j in range(n_tiles):
        cols = slice(j * LANES, (j + 1) * LANES)
        for p in range(SUBLANES):
            xin_ref[j, slab_rows(p), :] = xr_ref[seg_rows(p), cols]
        x = [xin_ref[j, pl.ds(t, SUBLANES, stride=pitch), :] for t in range(seg)]

        def shifted(t, s):
            if t >= s:
                return x[t - s]
            i = SUBLANES - s + t
            prev = jnp.broadcast_to(tail_ref[i:i + 1, cols], (SUBLANES, LANES))
            return jnp.where(sub >= 1, pltpu.roll(x[seg + t - s], 1, axis=0), prev)

        xc = []
        for t in range(seg):
            acc = cb_ref[:, cols] + cw_ref[CONV_WIDTH - 1:CONV_WIDTH, cols] * x[t]
            for s in range(1, CONV_WIDTH):
                k = CONV_WIDTH - 1 - s
                acc = acc + cw_ref[k:k + 1, cols] * shifted(t, s)
            xc.append(acc)
        xc = jnp.concatenate(xc, axis=0)
        xc_ref[:, cols] = xc
        xcb_ref[:, cols] = xc.astype(BF16)

    tail_ref[...] = xr_ref[tc - SUBLANES:tc, :]

    for j in range(n_tiles):
        cols = slice(j * LANES, (j + 1) * LANES)
        ks = starts[j] * LANES
        pre = jnp.dot(xcb_ref[:, ks:ks + GATE_WINDOW], wband_ref[j], preferred_element_type=F32)
        r = jax.nn.sigmoid(pre[:, :LANES] + ba_ref[:, cols])
        i = jax.nn.sigmoid(pre[:, LANES:] + bx_ref[:, cols])
        nlam = -lam_ref[:, cols]
        softplus = jnp.maximum(nlam, 0.0) + jnp.log1p(jnp.exp(-jnp.abs(nlam)))
        log_a = (-LRU_C) * r * softplus
        a = jnp.exp(log_a)
        y = -jnp.tanh(log_a) * (a * a + 1.0)
        u = y * lax.rsqrt(jnp.maximum(y, F32_TINY)) * (i * xc_ref[:, cols])
        a = a.reshape(seg, SUBLANES, LANES)
        u = u.reshape(seg, SUBLANES, LANES)

        h = u[0]
        decay = a[0]
        for t in range(1, seg):
            h = a[t] * h + u[t]
            decay = decay * a[t]
        for d in (1, 2, 4):
            m = sub >= d
            h = h + decay * jnp.where(m, pltpu.roll(h, d, axis=0), 0.0)
            decay = decay * jnp.where(m, pltpu.roll(decay, d, axis=0), 1.0)
        h_in = hcar_ref[:, cols]
        h_end = h + decay * h_in
        hcar_ref[:, cols] = jnp.broadcast_to(h_end[SUBLANES - 1:SUBLANES, :], (SUBLANES, LANES))
        h = jnp.where(sub >= 1, pltpu.roll(h_end, 1, axis=0), h_in)
        for t in range(seg):
            h = a[t] * h + u[t]
            hout_ref[j, pl.ds(t, SUBLANES, stride=pitch), :] = h
        for p in range(SUBLANES):
            y_ref[seg_rows(p), cols] = (gg_ref[seg_rows(p), cols] * hout_ref[j, slab_rows(p), :]).astype(y_ref.dtype)


def _rnn(xr, gg, conv_w, conv_b, wband, b_a, b_x, lam, tc=256):
    b, s, d_rnn = xr.shape
    tc = _tile(s, tc)
    starts = _gate_window_starts(d_rnn)
    n_tiles = d_rnn // LANES
    slab = (n_tiles, tc + SUBLANES * SUBLANES, LANES)
    row = lambda v: v.reshape(1, d_rnn)
    full2 = lambda shape: pl.BlockSpec(shape, lambda bi, ci: (0, 0))
    chunk = pl.BlockSpec((None, tc, d_rnn), lambda bi, ci: (bi, ci, 0))
    return pl.pallas_call(
        functools.partial(_rnn_kernel, tc=tc, d_rnn=d_rnn, starts=starts),
        grid=(b, s // tc),
        in_specs=[chunk, chunk,
                  full2((CONV_WIDTH, d_rnn)), full2((1, d_rnn)),
                  pl.BlockSpec(wband.shape, lambda bi, ci: (0, 0, 0)),
                  full2((1, d_rnn)), full2((1, d_rnn)), full2((1, d_rnn))],
        out_specs=chunk,
        out_shape=jax.ShapeDtypeStruct((b, s, d_rnn), BF16),
        scratch_shapes=[pltpu.VMEM((SUBLANES, d_rnn), F32),
                        pltpu.VMEM(slab, F32),
                        pltpu.VMEM(slab, F32),
                        pltpu.VMEM((tc, d_rnn), F32),
                        pltpu.VMEM((tc, d_rnn), BF16),
                        pltpu.VMEM((SUBLANES, d_rnn), F32)],
        compiler_params=_cparams("parallel", "arbitrary"),
        name="rnn",
    )(xr, gg, conv_w, row(conv_b), wband, row(b_a), row(b_x), row(lam))


PRUNE_LOG = -104.0


def _attn_kernel(q_ref, k_ref, v_ref, tri_ref, o_ref, *, tq, hp):
    qi = pl.program_id(2)
    row = lax.broadcasted_iota(jnp.int32, (tq, tq), 0)
    col = lax.broadcasted_iota(jnp.int32, (tq, tq), 1)
    causal = col < row

    def block(hd, kb, run, acc, diag):
        lanes = slice(hd * HEAD_DIM, (hd + 1) * HEAD_DIM)
        start = pl.multiple_of(kb * tq, tq)
        k = k_ref[pl.ds(start, tq), lanes]
        v = v_ref[pl.ds(start, tq), lanes]
        z = lax.dot_general(q_ref[:, lanes], k, (((1,), (1,)), ((), ())), preferred_element_type=F32)
        lk = -(jnp.maximum(z, 0.0) + jnp.log(1.0 + jnp.exp(-jnp.abs(z))))
        if diag:
            lk = jnp.where(causal, lk, 0.0)
        excl = jnp.dot(lk.astype(BF16), tri_ref[...], preferred_element_type=F32)
        w = jnp.exp(z + lk + excl + run)
        if diag:
            w = jnp.where(causal, w, 0.0)
        acc = acc + jnp.dot(w.astype(BF16), v, preferred_element_type=F32)
        run = run + jnp.sum(lk, axis=-1, keepdims=True)
        return run, acc

    def run_max(state):
        m = jnp.max(state[0][0])
        for run, _ in state[1:]:
            m = jnp.maximum(m, jnp.max(run))
        return m

    zero = (jnp.zeros((tq, 1), F32), jnp.zeros((tq, HEAD_DIM), F32))

    def store(state):
        for hd, (_, acc) in enumerate(state):
            o_ref[:, hd * HEAD_DIM:(hd + 1) * HEAD_DIM] = acc.astype(o_ref.dtype)

    @pl.when(qi == 0)
    def _():
        store(tuple(block(hd, qi, *zero, True) for hd in range(hp)))

    @pl.when(qi > 0)
    def _():
        state = tuple(block(hd, qi - 1, *block(hd, qi, *zero, True), False) for hd in range(hp))

        def cond(c):
            kb, rmax, _ = c
            return jnp.logical_and(kb >= 0, rmax > PRUNE_LOG)

        def body(c):
            kb, _, st = c
            st = tuple(block(hd, kb, run, acc, False) for hd, (run, acc) in enumerate(st))
            return kb - 1, run_max(st), st

        _, _, state = lax.while_loop(cond, body, (qi - 2, run_max(state), state))
        store(state)


def _attention(qk, v, tq=256, hp=4):
    b, s, _ = v.shape
    tq = _tile(s, tq)
    hw = hp * HEAD_DIM
    n_hg = N_HEADS // hp
    tri = jnp.asarray(np.tril(np.ones((tq, tq), np.float32), -1), BF16)
    return pl.pallas_call(
        functools.partial(_attn_kernel, tq=tq, hp=hp),
        grid=(b, n_hg, s // tq),
        in_specs=[pl.BlockSpec((None, tq, hw), lambda bi, h, qi: (bi, qi, h)),
                  pl.BlockSpec((None, s, hw), lambda bi, h, qi: (bi, 0, n_hg + h)),
                  pl.BlockSpec((None, s, hw), lambda bi, h, qi: (bi, 0, h)),
                  pl.BlockSpec((tq, tq), lambda bi, h, qi: (0, 0))],
        out_specs=pl.BlockSpec((None, tq, hw), lambda bi, h, qi: (bi, qi, h)),
        out_shape=jax.ShapeDtypeStruct(v.shape, BF16),
        compiler_params=_cparams("parallel", "parallel", "arbitrary"),
        name="attention",
    )(qk, qk, v, tri)


def _mix_kernel(yr_ref, oa_ref, wr_ref, wa_ref, sr_ref, sa_ref, o_ref):
    ya = jnp.dot(yr_ref[...], wr_ref[...], preferred_element_type=F32)
    yb = jnp.dot(oa_ref[...], wa_ref[...], preferred_element_type=F32)
    o_ref[...] = (sr_ref[...] * ya + sa_ref[...] * yb).astype(o_ref.dtype)


def _mix(yr, oa, w_r, w_a, gates, tm=1024, tn=1024):
    t, d_rnn = yr.shape
    d_attn = oa.shape[1]
    n = w_r.shape[1]
    tm = _tile(t, tm)
    tn = _tile(n, tn)
    nj = n // tn
    return pl.pallas_call(
        _mix_kernel,
        grid=(nj, t // tm),
        in_specs=[pl.BlockSpec((tm, d_rnn), lambda j, i: (i, 0)),
                  pl.BlockSpec((tm, d_attn), lambda j, i: (i, 0)),
                  pl.BlockSpec((d_rnn, tn), lambda j, i: (0, j), pipeline_mode=pl.Buffered(1)),
                  pl.BlockSpec((d_attn, tn), lambda j, i: (0, j), pipeline_mode=pl.Buffered(1)),
                  pl.BlockSpec((tm, tn), lambda j, i: (i, j)),
                  pl.BlockSpec((tm, tn), lambda j, i: (i, nj + j))],
        out_specs=pl.BlockSpec((tm, tn), lambda j, i: (i, j)),
        out_shape=jax.ShapeDtypeStruct((t, n), BF16),
        compiler_params=_cparams("parallel", "parallel"),
        name="mix",
    )(yr, oa, w_r, w_a, gates, gates)


def _oproj_kernel(m_ref, w_ref, x_ref, g_ref, x1_ref, h_ref):
    x1 = x_ref[...] + jnp.dot(m_ref[...], w_ref[...], preferred_element_type=F32)
    x1_ref[...] = x1
    ms = jnp.mean(x1 * x1, axis=-1, keepdims=True)
    h_ref[...] = (x1 * lax.rsqrt(ms + EPS) * g_ref[...]).astype(h_ref.dtype)


def _oproj(mix, w_o, x, g_next, tm=512):
    t, d = x.shape
    tm = _tile(t, tm)
    return pl.pallas_call(
        _oproj_kernel,
        grid=(t // tm,),
        in_specs=[pl.BlockSpec((tm, d), lambda i: (i, 0)),
                  pl.BlockSpec((d, d), lambda i: (0, 0)),
                  pl.BlockSpec((tm, d), lambda i: (i, 0)),
                  pl.BlockSpec((1, d), lambda i: (0, 0))],
        out_specs=[pl.BlockSpec((tm, d), lambda i: (i, 0)),
                   pl.BlockSpec((tm, d), lambda i: (i, 0))],
        out_shape=[jax.ShapeDtypeStruct((t, d), F32), jax.ShapeDtypeStruct((t, d), BF16)],
        compiler_params=_cparams("parallel"),
        name="oproj",
    )(mix, w_o, x, g_next.reshape(1, d))


def _ffn_kernel(h_ref, wg_ref, wu_ref, wd_ref, x_ref, g_ref, x2_ref, h3_ref, acc_ref):
    f = pl.program_id(1)

    @pl.when(f == 0)
    def _():
        acc_ref[...] = x_ref[...]

    h = h_ref[...]
    g = jnp.dot(h, wg_ref[...], preferred_element_type=F32)
    u = jnp.dot(h, wu_ref[...], preferred_element_type=F32)
    act = (jax.nn.silu(g) * u).astype(BF16)
    acc_ref[...] += jnp.dot(act, wd_ref[...], preferred_element_type=F32)

    @pl.when(f == pl.num_programs(1) - 1)
    def _():
        x2 = acc_ref[...]
        x2_ref[...] = x2
        ms = jnp.mean(x2 * x2, axis=-1, keepdims=True)
        h3_ref[...] = (x2 * lax.rsqrt(ms + EPS) * g_ref[...]).astype(h3_ref.dtype)


def _ffn(h2, w_g, w_u, w_d, x1, g_next, tm=512, tf=512):
    t, d = x1.shape
    d_ff = w_g.shape[1]
    tm = _tile(t, tm)
    tf = _tile(d_ff, tf)
    return pl.pallas_call(
        _ffn_kernel,
        grid=(t // tm, d_ff // tf),
        in_specs=[pl.BlockSpec((tm, d), lambda i, f: (i, 0)),
                  pl.BlockSpec((d, tf), lambda i, f: (0, f)),
                  pl.BlockSpec((d, tf), lambda i, f: (0, f)),
                  pl.BlockSpec((tf, d), lambda i, f: (f, 0)),
                  pl.BlockSpec((tm, d), lambda i, f: (i, 0)),
                  pl.BlockSpec((1, d), lambda i, f: (0, 0))],
        out_specs=[pl.BlockSpec((tm, d), lambda i, f: (i, 0)),
                   pl.BlockSpec((tm, d), lambda i, f: (i, 0))],
        out_shape=[jax.ShapeDtypeStruct((t, d), F32), jax.ShapeDtypeStruct((t, d), BF16)],
        scratch_shapes=[pltpu.VMEM((tm, d), F32)],
        compiler_params=_cparams("parallel", "arbitrary"),
        name="ffn",
    )(h2, w_g, w_u, w_d, x1, g_next.reshape(1, d))


def _ple_kernel(h_ref, wg_ref, p_ref, wp_ref, x_ref, o_ref):
    gate = jax.nn.sigmoid(jnp.dot(h_ref[...], wg_ref[...], preferred_element_type=F32))
    emb = jnp.dot(p_ref[...], wp_ref[...], preferred_element_type=F32)
    o_ref[...] = x_ref[...] + gate * emb


def _ple(h3, w_gate, p, w_proj, x2, tm=1024, tn=1024):
    t, d = x2.shape
    dp = p.shape[1]
    tm = _tile(t, tm)
    tn = _tile(d, tn)
    return pl.pallas_call(
        _ple_kernel,
        grid=(t // tm, d // tn),
        in_specs=[pl.BlockSpec((tm, d), lambda i, j: (i, 0)),
                  pl.BlockSpec((d, tn), lambda i, j: (0, j)),
                  pl.BlockSpec((tm, dp), lambda i, j: (i, 0)),
                  pl.BlockSpec((dp, tn), lambda i, j: (0, j)),
                  pl.BlockSpec((tm, tn), lambda i, j: (i, j))],
        out_specs=pl.BlockSpec((tm, tn), lambda i, j: (i, j)),
        out_shape=jax.ShapeDtypeStruct((t, d), F32),
        compiler_params=_cparams("parallel", "parallel"),
        name="ple",
    )(h3, w_gate, p, w_proj, x2)


def _layer(x, p, g_mix, w_in, conv_w, conv_b, w_rg_a, b_rg_a, w_rg_x, b_rg_x, lru_lambda, q_gain,
           k_gain, w_rnn_out, w_attn_out, w_o, g_ffn, w_ffn_gu, w_ffn_down, g_ple, w_ple_gate,
           w_ple_proj):
    b, s, d = x.shape
    t = b * s
    d_rnn = w_rnn_out.shape[0]
    d_attn = w_attn_out.shape[0]
    d_ff = w_ffn_down.shape[0]
    x2d = x.reshape(t, d)

    o0 = 0
    o1 = o0 + d_rnn
    o2 = o1 + d_rnn
    o3 = o2 + 2 * d_attn
    o4 = o3 + d_attn
    o5 = o4 + 2 * d
    assert o5 == w_in.shape[1]
    wb = lambda w: w.astype(BF16)

    h = _rmsnorm(x2d, g_mix)
    xr = _proj(h, w_in, o0, d_rnn, "none", F32, tn=d_rnn // 2)
    gg = _proj(h, w_in, o1, d_rnn, "gelu", F32, tn=d_rnn // 2)
    qk_gain = jnp.concatenate([jnp.tile(q_gain * (1.0 / math.sqrt(HEAD_DIM)), N_HEADS),
                               jnp.tile(k_gain, N_HEADS)]).reshape(1, 2 * d_attn)
    qk = _proj(h, w_in, o2, 2 * d_attn, "headnorm", BF16, gain=qk_gain)
    v = _proj(h, w_in, o3, d_attn, "none", BF16)
    gates = _proj(h, w_in, o4, 2 * d, "sigmoid", BF16)

    wband = _band_gate_weights(w_rg_a, w_rg_x)
    yr = _rnn(xr.reshape(b, s, d_rnn), gg.reshape(b, s, d_rnn), conv_w, conv_b, wband,
              b_rg_a, b_rg_x, lru_lambda)
    oa = _attention(qk.reshape(b, s, 2 * d_attn), v.reshape(b, s, d_attn))

    mix = _mix(yr.reshape(t, d_rnn), oa.reshape(t, d_attn), wb(w_rnn_out), wb(w_attn_out), gates)
    x1, h2 = _oproj(mix, wb(w_o), x2d, g_ffn)
    x2, h3 = _ffn(h2, wb(w_ffn_gu[:, :d_ff]), wb(w_ffn_gu[:, d_ff:]), wb(w_ffn_down), x1, g_ple)
    out = _ple(h3, wb(w_ple_gate), wb(p.reshape(t, -1)), wb(w_ple_proj), x2)
    return out.reshape(b, s, d)


def kernel(x, p, g_mix, w_in, conv_w, conv_b, w_rg_a, b_rg_a, w_rg_x, b_rg_x, lru_lambda, q_gain,
           k_gain, w_rnn_out, w_attn_out, w_o, g_ffn, w_ffn_gu, w_ffn_down, g_ple, w_ple_gate,
           w_ple_proj):
    params = (g_mix, w_in, conv_w, conv_b, w_rg_a, b_rg_a, w_rg_x, b_rg_x, lru_lambda, q_gain,
              k_gain, w_rnn_out, w_attn_out, w_o, g_ffn, w_ffn_gu, w_ffn_down, g_ple, w_ple_gate,
              w_ple_proj)
    for i in range(p.shape[0]):
        x = _layer(x, p[i], *[a[i] for a in params])
    return x
```

```python
import functools
import math

import jax
import jax.numpy as jnp
import numpy as np
from jax import lax
from jax.experimental import pallas as pl
from jax.experimental.pallas import tpu as pltpu

F32 = jnp.float32
BF16 = jnp.bfloat16

EPS = 1e-6
LRU_C = 8.0
N_HEADS = 16
HEAD_DIM = 128
N_RNN_BLOCKS = 16
CONV_WIDTH = 4

LANES = 128
SUBLANES = 8
GATE_WINDOW = 4 * LANES
VMEM_LIMIT = 56 * 1024 * 1024
F32_TINY = float(np.finfo(np.float32).tiny)


def _cparams(*sem):
    return pltpu.CompilerParams(dimension_semantics=sem, vmem_limit_bytes=VMEM_LIMIT)


def _tile(n, pref):
    t = min(n, pref)
    assert n % t == 0, (n, t)
    return t


def _rmsnorm_kernel(x_ref, g_ref, o_ref):
    x = x_ref[...]
    ms = jnp.mean(x * x, axis=-1, keepdims=True)
    o_ref[...] = (x * lax.rsqrt(ms + EPS) * g_ref[...]).astype(o_ref.dtype)


def _rmsnorm(x, g, tm=512):
    t, d = x.shape
    tm = _tile(t, tm)
    return pl.pallas_call(
        _rmsnorm_kernel,
        grid=(t // tm,),
        in_specs=[pl.BlockSpec((tm, d), lambda i: (i, 0)),
                  pl.BlockSpec((1, d), lambda i: (0, 0))],
        out_specs=pl.BlockSpec((tm, d), lambda i: (i, 0)),
        out_shape=jax.ShapeDtypeStruct((t, d), BF16),
        compiler_params=_cparams("parallel"),
        name="rmsnorm",
    )(x, g.reshape(1, d))


def _head_norm(acc, gain):
    outs = []
    for c in range(acc.shape[1] // HEAD_DIM):
        blk = acc[:, c * HEAD_DIM:(c + 1) * HEAD_DIM]
        ms = jnp.mean(blk * blk, axis=-1, keepdims=True)
        outs.append(blk * lax.rsqrt(ms + EPS) * gain[:, c * HEAD_DIM:(c + 1) * HEAD_DIM])
    return jnp.concatenate(outs, axis=1)


PROJ_ROW_SLABS = 4


def _proj_kernel(h_ref, w_ref, *rest, kind):
    *extra, o_ref, wb_ref = rest

    @pl.when(pl.program_id(1) == 0)
    def _():
        wb_ref[...] = w_ref[...].astype(BF16)

    rows = h_ref.shape[0] // PROJ_ROW_SLABS
    for m in range(PROJ_ROW_SLABS):
        sl = slice(m * rows, (m + 1) * rows)
        acc = jnp.dot(h_ref[sl, :], wb_ref[...], preferred_element_type=F32)
        if kind == "gelu":
            acc = jax.nn.gelu(acc)
        elif kind == "sigmoid":
            acc = jax.nn.sigmoid(acc)
        elif kind == "headnorm":
            acc = _head_norm(acc, extra[0][...])
        o_ref[sl, :] = acc.astype(o_ref.dtype)


def _proj(h, w, col0, n, kind, out_dtype, gain=None, tm=1024, tn=1024):
    t, d = h.shape
    tm = _tile(t, tm)
    tn = _tile(n, tn)
    assert col0 % LANES == 0 and tn % LANES == 0 and tm % (PROJ_ROW_SLABS * 2 * SUBLANES) == 0
    in_specs = [pl.BlockSpec((tm, d), lambda j, i: (i, 0)),
                pl.BlockSpec((pl.Element(d), pl.Element(tn)),
                             lambda j, i: (0, pl.multiple_of(col0 + j * tn, LANES)))]
    args = [h, w]
    if gain is not None:
        in_specs.append(pl.BlockSpec((1, tn), lambda j, i: (0, j)))
        args.append(gain)
    return pl.pallas_call(
        functools.partial(_proj_kernel, kind=kind),
        grid=(n // tn, t // tm),
        in_specs=in_specs,
        out_specs=pl.BlockSpec((tm, tn), lambda j, i: (i, j)),
        out_shape=jax.ShapeDtypeStruct((t, n), out_dtype),
        scratch_shapes=[pltpu.VMEM((d, tn), BF16)],
        compiler_params=_cparams("parallel", "arbitrary"),
        name="proj_" + kind,
    )(*args)


def _gate_window_starts(d_rnn):
    blk = d_rnn // N_RNN_BLOCKS
    n_tiles = d_rnn // LANES
    starts = []
    for j in range(n_tiles):
        b0 = (LANES * j) // blk
        b1 = (LANES * j + LANES - 1) // blk
        s = min((blk * b0) // LANES, n_tiles - GATE_WINDOW // LANES)
        assert s * LANES <= blk * b0 and blk * (b1 + 1) <= s * LANES + GATE_WINDOW
        starts.append(s)
    return starts


def _band_gate_weights(w_a, w_x):
    nb, blk, _ = w_a.shape
    d_rnn = nb * blk
    starts = _gate_window_starts(d_rnn)

    def band(w):
        tiles = []
        for j, s in enumerate(starts):
            c_lo, c_hi = LANES * j, LANES * (j + 1)
            pieces = []
            for b in range(c_lo // blk, (c_hi - 1) // blk + 1):
                oc0, oc1 = max(c_lo, blk * b), min(c_hi, blk * (b + 1))
                r0 = blk * b - s * LANES
                pieces.append(jnp.pad(w[b, :, oc0 - blk * b:oc1 - blk * b],
                                      ((r0, GATE_WINDOW - blk - r0), (0, 0))))
            tiles.append(jnp.concatenate(pieces, axis=1))
        return jnp.stack(tiles)

    return jnp.concatenate([band(w_a), band(w_x)], axis=-1).astype(BF16)


def _rnn_kernel(xr_ref, gg_ref, cw_ref, cb_ref, wband_ref, ba_ref, bx_ref, lam_ref, y_ref,
                tail_ref, xin_ref, hout_ref, xc_ref, xcb_ref, hcar_ref, *, tc, d_rnn, starts):
    n_tiles = d_rnn // LANES
    seg = tc // SUBLANES
    pitch = seg + SUBLANES

    @pl.when(pl.program_id(1) == 0)
    def _():
        tail_ref[...] = jnp.zeros_like(tail_ref)
        hcar_ref[...] = jnp.zeros_like(hcar_ref)

    sub = lax.broadcasted_iota(jnp.int32, (SUBLANES, LANES), 0)

    def seg_rows(p):
        return slice(p * seg, (p + 1) * seg)

    def slab_rows(p):
        return slice(p * pitch, p * pitch + seg)

    for j in range(n_tiles):
        cols = slice(j * LANES, (j + 1) * LANES)
        for p in range(SUBLANES):
            xin_ref[j, slab_rows(p), :] = xr_ref[seg_rows(p), cols]
        x = [xin_ref[j, pl.ds(t, SUBLANES, stride=pitch), :] for t in range(seg)]

        def shifted(t, s):
            if t >= s:
                return x[t - s]
            i = SUBLANES - s + t
            prev = jnp.broadcast_to(tail_ref[i:i + 1, cols], (SUBLANES, LANES))
            return jnp.where(sub >= 1, pltpu.roll(x[seg + t - s], 1, axis=0), prev)

        xc = []
        for t in range(seg):
            acc = cb_ref[:, cols] + cw_ref[CONV_WIDTH - 1:CONV_WIDTH, cols] * x[t]
            for s in range(1, CONV_WIDTH):
                k = CONV_WIDTH - 1 - s
                acc = acc + cw_ref[k:k + 1, cols] * shifted(t, s)
            xc.append(acc)
        xc = jnp.concatenate(xc, axis=0)
        xc_ref[:, cols] = xc
        xcb_ref[:, cols] = xc.astype(BF16)

    tail_ref[...] = xr_ref[tc - SUBLANES:tc, :]

    for j in range(n_tiles):
        cols = slice(j * LANES, (j + 1) * LANES)
        ks = starts[j] * LANES
        pre = jnp.dot(xcb_ref[:, ks:ks + GATE_WINDOW], wband_ref[j], preferred_element_type=F32)
        r = 0.5 * jnp.tanh(0.5 * (pre[:, :LANES] + ba_ref[:, cols])) + 0.5
        i = 0.5 * jnp.tanh(0.5 * (pre[:, LANES:] + bx_ref[:, cols])) + 0.5
        nlam = -lam_ref[:, cols]
        softplus = jnp.maximum(nlam, 0.0) + jnp.log1p(jnp.exp(-jnp.abs(nlam)))
        log_a = r * ((-LRU_C) * softplus)
        a = jnp.exp(log_a)
        y = -jnp.tanh(log_a) * (a * a + 1.0)
        u = y * lax.rsqrt(jnp.maximum(y, F32_TINY)) * (i * xc_ref[:, cols])
        a = a.reshape(seg, SUBLANES, LANES)
        u = u.reshape(seg, SUBLANES, LANES)

        h = u[0]
        decay = a[0]
        for t in range(1, seg):
            h = a[t] * h + u[t]
            decay = decay * a[t]
        for d in (1, 2, 4):
            m = sub >= d
            h = h + decay * jnp.where(m, pltpu.roll(h, d, axis=0), 0.0)
            decay = decay * jnp.where(m, pltpu.roll(decay, d, axis=0), 1.0)
        h_in = hcar_ref[:, cols]
        h_end = h + decay * h_in
        hcar_ref[:, cols] = jnp.broadcast_to(h_end[SUBLANES - 1:SUBLANES, :], (SUBLANES, LANES))
        h = jnp.where(sub >= 1, pltpu.roll(h_end, 1, axis=0), h_in)
        for t in range(seg):
            h = a[t] * h + u[t]
            hout_ref[j, pl.ds(t, SUBLANES, stride=pitch), :] = h
        for p in range(SUBLANES):
            y_ref[seg_rows(p), cols] = (gg_ref[seg_rows(p), cols] * hout_ref[j, slab_rows(p), :]).astype(y_ref.dtype)


def _rnn(xr, gg, conv_w, conv_b, wband, b_a, b_x, lam, tc=256):
    b, s, d_rnn = xr.shape
    tc = _tile(s, tc)
    starts = _gate_window_starts(d_rnn)
    n_tiles = d_rnn // LANES
    slab = (n_tiles, tc + SUBLANES * SUBLANES, LANES)
    row = lambda v: v.reshape(1, d_rnn)
    full2 = lambda shape: pl.BlockSpec(shape, lambda bi, ci: (0, 0))
    chunk = pl.BlockSpec((None, tc, d_rnn), lambda bi, ci: (bi, ci, 0))
    return pl.pallas_call(
        functools.partial(_rnn_kernel, tc=tc, d_rnn=d_rnn, starts=starts),
        grid=(b, s // tc),
        in_specs=[chunk, chunk,
                  full2((CONV_WIDTH, d_rnn)), full2((1, d_rnn)),
                  pl.BlockSpec(wband.shape, lambda bi, ci: (0, 0, 0)),
                  full2((1, d_rnn)), full2((1, d_rnn)), full2((1, d_rnn))],
        out_specs=chunk,
        out_shape=jax.ShapeDtypeStruct((b, s, d_rnn), BF16),
        scratch_shapes=[pltpu.VMEM((SUBLANES, d_rnn), F32),
                        pltpu.VMEM(slab, F32),
                        pltpu.VMEM(slab, F32),
                        pltpu.VMEM((tc, d_rnn), F32),
                        pltpu.VMEM((tc, d_rnn), BF16),
                        pltpu.VMEM((SUBLANES, d_rnn), F32)],
        compiler_params=_cparams("parallel", "arbitrary"),
        name="rnn",
    )(xr, gg, conv_w, row(conv_b), wband, row(b_a), row(b_x), row(lam))


PRUNE_LOG = -104.0
SOFTPLUS_LINEAR = 80.0


def _attn_kernel(q_ref, k_ref, v_ref, tri_ref, o_ref, *, tq, hp):
    qi = pl.program_id(2)
    row = lax.broadcasted_iota(jnp.int32, (tq, tq), 0)
    col = lax.broadcasted_iota(jnp.int32, (tq, tq), 1)
    causal = col < row

    def block(hd, kb, run, acc, diag):
        lanes = slice(hd * HEAD_DIM, (hd + 1) * HEAD_DIM)
        start = pl.multiple_of(kb * tq, tq)
        k = k_ref[pl.ds(start, tq), lanes]
        v = v_ref[pl.ds(start, tq), lanes]
        z = lax.dot_general(q_ref[:, lanes], k, (((1,), (1,)), ((), ())), preferred_element_type=F32)
        sp = jnp.where(z > SOFTPLUS_LINEAR, z, jnp.log(1.0 + jnp.exp(z)))
        if diag:
            sp = jnp.where(causal, sp, 0.0)
        excl = jnp.dot(sp.astype(BF16), tri_ref[...], preferred_element_type=F32)
        w = jnp.exp((z - sp) + excl + run)
        if diag:
            w = jnp.where(causal, w, 0.0)
        acc = acc + jnp.dot(w.astype(BF16), v, preferred_element_type=F32)
        run = run - jnp.sum(sp, axis=-1, keepdims=True)
        return run, acc

    def run_max(state):
        m = jnp.max(state[0][0])
        for run, _ in state[1:]:
            m = jnp.maximum(m, jnp.max(run))
        return m

    zero = (jnp.zeros((tq, 1), F32), jnp.zeros((tq, HEAD_DIM), F32))

    def store(state):
        for hd, (_, acc) in enumerate(state):
            o_ref[:, hd * HEAD_DIM:(hd + 1) * HEAD_DIM] = acc.astype(o_ref.dtype)

    @pl.when(qi == 0)
    def _():
        store(tuple(block(hd, qi, *zero, True) for hd in range(hp)))

    @pl.when(qi > 0)
    def _():
        state = tuple(block(hd, qi - 1, *block(hd, qi, *zero, True), False) for hd in range(hp))

        def cond(c):
            kb, rmax, _ = c
            return jnp.logical_and(kb >= 0, rmax > PRUNE_LOG)

        def body(c):
            kb, _, st = c
            st = tuple(block(hd, kb, run, acc, False) for hd, (run, acc) in enumerate(st))
            return kb - 1, run_max(st), st

        _, _, state = lax.while_loop(cond, body, (qi - 2, run_max(state), state))
        store(state)


def _attention(qk, v, tq=256, hp=4):
    b, s, _ = v.shape
    tq = _tile(s, tq)
    hw = hp * HEAD_DIM
    n_hg = N_HEADS // hp
    tri = jnp.asarray(-np.tril(np.ones((tq, tq), np.float32), -1), BF16)
    return pl.pallas_call(
        functools.partial(_attn_kernel, tq=tq, hp=hp),
        grid=(b, n_hg, s // tq),
        in_specs=[pl.BlockSpec((None, tq, hw), lambda bi, h, qi: (bi, qi, h)),
                  pl.BlockSpec((None, s, hw), lambda bi, h, qi: (bi, 0, n_hg + h)),
                  pl.BlockSpec((None, s, hw), lambda bi, h, qi: (bi, 0, h)),
                  pl.BlockSpec((tq, tq), lambda bi, h, qi: (0, 0))],
        out_specs=pl.BlockSpec((None, tq, hw), lambda bi, h, qi: (bi, qi, h)),
        out_shape=jax.ShapeDtypeStruct(v.shape, BF16),
        compiler_params=_cparams("parallel", "parallel", "arbitrary"),
        name="attention",
    )(qk, qk, v, tri)


def _mix_kernel(yr_ref, oa_ref, wr_ref, wa_ref, sr_ref, sa_ref, o_ref):
    ya = jnp.dot(yr_ref[...], wr_ref[...], preferred_element_type=F32)
    yb = jnp.dot(oa_ref[...], wa_ref[...], preferred_element_type=F32)
    o_ref[...] = (sr_ref[...] * ya + sa_ref[...] * yb).astype(o_ref.dtype)


def _mix(yr, oa, w_r, w_a, gates, tm=1024, tn=1024):
    t, d_rnn = yr.shape
    d_attn = oa.shape[1]
    n = w_r.shape[1]
    tm = _tile(t, tm)
    tn = _tile(n, tn)
    nj = n // tn
    return pl.pallas_call(
        _mix_kernel,
        grid=(nj, t // tm),
        in_specs=[pl.BlockSpec((tm, d_rnn), lambda j, i: (i, 0)),
                  pl.BlockSpec((tm, d_attn), lambda j, i: (i, 0)),
                  pl.BlockSpec((d_rnn, tn), lambda j, i: (0, j), pipeline_mode=pl.Buffered(1)),
                  pl.BlockSpec((d_attn, tn), lambda j, i: (0, j), pipeline_mode=pl.Buffered(1)),
                  pl.BlockSpec((tm, tn), lambda j, i: (i, j)),
                  pl.BlockSpec((tm, tn), lambda j, i: (i, nj + j))],
        out_specs=pl.BlockSpec((tm, tn), lambda j, i: (i, j)),
        out_shape=jax.ShapeDtypeStruct((t, n), BF16),
        compiler_params=_cparams("parallel", "parallel"),
        name="mix",
    )(yr, oa, w_r, w_a, gates, gates)


def _oproj_kernel(m_ref, w_ref, x_ref, g_ref, x1_ref, h_ref):
    x1 = x_ref[...] + jnp.dot(m_ref[...], w_ref[...], preferred_element_type=F32)
    x1_ref[...] = x1
    ms = jnp.mean(x1 * x1, axis=-1, keepdims=True)
    h_ref[...] = (x1 * lax.rsqrt(ms + EPS) * g_ref[...]).astype(h_ref.dtype)


def _oproj(mix, w_o, x, g_next, tm=512):
    t, d = x.shape
    tm = _tile(t, tm)
    return pl.pallas_call(
        _oproj_kernel,
        grid=(t // tm,),
        in_specs=[pl.BlockSpec((tm, d), lambda i: (i, 0)),
                  pl.BlockSpec((d, d), lambda i: (0, 0)),
                  pl.BlockSpec((tm, d), lambda i: (i, 0)),
                  pl.BlockSpec((1, d), lambda i: (0, 0))],
        out_specs=[pl.BlockSpec((tm, d), lambda i: (i, 0)),
                   pl.BlockSpec((tm, d), lambda i: (i, 0))],
        out_shape=[jax.ShapeDtypeStruct((t, d), F32), jax.ShapeDtypeStruct((t, d), BF16)],
        compiler_params=_cparams("parallel"),
        name="oproj",
    )(mix, w_o, x, g_next.reshape(1, d))


def _ffn_kernel(h_ref, wgu_ref, wd_ref, x_ref, g_ref, x2_ref, h3_ref, *, tf):
    f = pl.program_id(1)

    @pl.when(f == 0)
    def _():
        x2_ref[...] = x_ref[...]

    gu = jnp.dot(h_ref[...], wgu_ref[...], preferred_element_type=F32)
    act = (jax.nn.silu(gu[:, :tf]) * gu[:, tf:]).astype(BF16)
    x2_ref[...] += jnp.dot(act, wd_ref[...], preferred_element_type=F32)

    @pl.when(f == pl.num_programs(1) - 1)
    def _():
        x2 = x2_ref[...]
        ms = jnp.mean(x2 * x2, axis=-1, keepdims=True)
        h3_ref[...] = (x2 * lax.rsqrt(ms + EPS) * g_ref[...]).astype(h3_ref.dtype)


def _interleave_gate_up(w_gu, tf):
    d, two_ff = w_gu.shape
    nf = two_ff // (2 * tf)
    return w_gu.reshape(d, 2, nf, tf).transpose(0, 2, 1, 3).reshape(d, two_ff).astype(BF16)


def _ffn(h2, w_gu, w_d, x1, g_next, tm=512, tf=512):
    t, d = x1.shape
    d_ff = w_d.shape[0]
    tm = _tile(t, tm)
    tf = _tile(d_ff, tf)
    return pl.pallas_call(
        functools.partial(_ffn_kernel, tf=tf),
        grid=(t // tm, d_ff // tf),
        in_specs=[pl.BlockSpec((tm, d), lambda i, f: (i, 0)),
                  pl.BlockSpec((d, 2 * tf), lambda i, f: (0, f)),
                  pl.BlockSpec((tf, d), lambda i, f: (f, 0)),
                  pl.BlockSpec((tm, d), lambda i, f: (i, 0)),
                  pl.BlockSpec((1, d), lambda i, f: (0, 0))],
        out_specs=[pl.BlockSpec((tm, d), lambda i, f: (i, 0)),
                   pl.BlockSpec((tm, d), lambda i, f: (i, 0))],
        out_shape=[jax.ShapeDtypeStruct((t, d), F32), jax.ShapeDtypeStruct((t, d), BF16)],
        compiler_params=_cparams("parallel", "arbitrary"),
        name="ffn",
    )(h2, _interleave_gate_up(w_gu, tf), w_d, x1, g_next.reshape(1, d))


def _ple_kernel(h_ref, wg_ref, p_ref, wp_ref, x_ref, o_ref):
    gate = jax.nn.sigmoid(jnp.dot(h_ref[...], wg_ref[...], preferred_element_type=F32))
    emb = jnp.dot(p_ref[...], wp_ref[...], preferred_element_type=F32)
    o_ref[...] = x_ref[...] + gate * emb


def _ple(h3, w_gate, p, w_proj, x2, tm=1024, tn=1024):
    t, d = x2.shape
    dp = p.shape[1]
    tm = _tile(t, tm)
    tn = _tile(d, tn)
    return pl.pallas_call(
        _ple_kernel,
        grid=(t // tm, d // tn),
        in_specs=[pl.BlockSpec((tm, d), lambda i, j: (i, 0)),
                  pl.BlockSpec((d, tn), lambda i, j: (0, j)),
                  pl.BlockSpec((tm, dp), lambda i, j: (i, 0)),
                  pl.BlockSpec((dp, tn), lambda i, j: (0, j)),
                  pl.BlockSpec((tm, tn), lambda i, j: (i, j))],
        out_specs=pl.BlockSpec((tm, tn), lambda i, j: (i, j)),
        out_shape=jax.ShapeDtypeStruct((t, d), F32),
        compiler_params=_cparams("parallel", "parallel"),
        name="ple",
    )(h3, w_gate, p, w_proj, x2)


def _layer(x, p, g_mix, w_in, conv_w, conv_b, w_rg_a, b_rg_a, w_rg_x, b_rg_x, lru_lambda, q_gain,
           k_gain, w_rnn_out, w_attn_out, w_o, g_ffn, w_ffn_gu, w_ffn_down, g_ple, w_ple_gate,
           w_ple_proj):
    b, s, d = x.shape
    t = b * s
    d_rnn = w_rnn_out.shape[0]
    d_attn = w_attn_out.shape[0]
    x2d = x.reshape(t, d)

    o0 = 0
    o1 = o0 + d_rnn
    o2 = o1 + d_rnn
    o3 = o2 + 2 * d_attn
    o4 = o3 + d_attn
    o5 = o4 + 2 * d
    assert o5 == w_in.shape[1]
    wb = lambda w: w.astype(BF16)

    h = _rmsnorm(x2d, g_mix)
    xr = _proj(h, w_in, o0, d_rnn, "none", F32, tn=d_rnn // 2)
    gg = _proj(h, w_in, o1, d_rnn, "gelu", F32, tn=d_rnn // 2)
    qk_gain = jnp.concatenate([jnp.tile(q_gain * (1.0 / math.sqrt(HEAD_DIM)), N_HEADS),
                               jnp.tile(k_gain, N_HEADS)]).reshape(1, 2 * d_attn)
    qk = _proj(h, w_in, o2, 2 * d_attn, "headnorm", BF16, gain=qk_gain)
    v = _proj(h, w_in, o3, d_attn, "none", BF16)
    gates = _proj(h, w_in, o4, 2 * d, "sigmoid", BF16)

    wband = _band_gate_weights(w_rg_a, w_rg_x)
    yr = _rnn(xr.reshape(b, s, d_rnn), gg.reshape(b, s, d_rnn), conv_w, conv_b, wband,
              b_rg_a, b_rg_x, lru_lambda)
    oa = _attention(qk.reshape(b, s, 2 * d_attn), v.reshape(b, s, d_attn))

    mix = _mix(yr.reshape(t, d_rnn), oa.reshape(t, d_attn), wb(w_rnn_out), wb(w_attn_out), gates)
    x1, h2 = _oproj(mix, wb(w_o), x2d, g_ffn)
    x2, h3 = _ffn(h2, w_ffn_gu, wb(w_ffn_down), x1, g_ple)
    out = _ple(h3, wb(w_ple_gate), wb(p.reshape(t, -1)), wb(w_ple_proj), x2)
    return out.reshape(b, s, d)


def kernel(x, p, g_mix, w_in, conv_w, conv_b, w_rg_a, b_rg_a, w_rg_x, b_rg_x, lru_lambda, q_gain,
           k_gain, w_rnn_out, w_attn_out, w_o, g_ffn, w_ffn_gu, w_ffn_down, g_ple, w_ple_gate,
           w_ple_proj):
    params = (g_mix, w_in, conv_w, conv_b, w_rg_a, b_rg_a, w_rg_x, b_rg_x, lru_lambda, q_gain,
              k_gain, w_rnn_out, w_attn_out, w_o, g_ffn, w_ffn_gu, w_ffn_down, g_ple, w_ple_gate,
              w_ple_proj)
    for i in range(p.shape[0]):
        x = _layer(x, p[i], *[a[i] for a in params])
    return x
```

```python
import functools
import math

import jax
import jax.numpy as jnp
import numpy as np
from jax import lax
from jax.experimental import pallas as pl
from jax.experimental.pallas import tpu as pltpu

F32 = jnp.float32
BF16 = jnp.bfloat16

EPS = 1e-6
LRU_C = 8.0
N_HEADS = 16
HEAD_DIM = 128
N_RNN_BLOCKS = 16
CONV_WIDTH = 4

LANES = 128
SUBLANES = 8
GATE_WINDOW = 4 * LANES
VMEM_LIMIT = 56 * 1024 * 1024
F32_TINY = float(np.finfo(np.float32).tiny)


def _cparams(*sem):
    return pltpu.CompilerParams(dimension_semantics=sem, vmem_limit_bytes=VMEM_LIMIT)


def _tile(n, pref):
    t = min(n, pref)
    assert n % t == 0, (n, t)
    return t


def _rmsnorm_kernel(x_ref, g_ref, o_ref):
    x = x_ref[...]
    ms = jnp.mean(x * x, axis=-1, keepdims=True)
    o_ref[...] = (x * lax.rsqrt(ms + EPS) * g_ref[...]).astype(o_ref.dtype)


def _rmsnorm(x, g, tm=512):
    t, d = x.shape
    tm = _tile(t, tm)
    return pl.pallas_call(
        _rmsnorm_kernel,
        grid=(t // tm,),
        in_specs=[pl.BlockSpec((tm, d), lambda i: (i, 0)),
                  pl.BlockSpec((1, d), lambda i: (0, 0))],
        out_specs=pl.BlockSpec((tm, d), lambda i: (i, 0)),
        out_shape=jax.ShapeDtypeStruct((t, d), BF16),
        compiler_params=_cparams("parallel"),
        name="rmsnorm",
    )(x, g.reshape(1, d))


def _head_norm(acc, gain):
    outs = []
    for c in range(acc.shape[1] // HEAD_DIM):
        blk = acc[:, c * HEAD_DIM:(c + 1) * HEAD_DIM]
        ms = jnp.mean(blk * blk, axis=-1, keepdims=True)
        outs.append(blk * lax.rsqrt(ms + EPS) * gain[:, c * HEAD_DIM:(c + 1) * HEAD_DIM])
    return jnp.concatenate(outs, axis=1)


PROJ_ROW_SLABS = 4


def _proj_kernel(h_ref, w_ref, *rest, kind, n_cast):
    wb_ref = rest[-1]
    cast_out = rest[len(rest) - 1 - n_cast:-1]
    o_ref = rest[-2 - n_cast]
    cast_in = rest[-2 - 2 * n_cast:-2 - n_cast]
    extra = rest[:-2 - 2 * n_cast]

    @pl.when(pl.program_id(1) == 0)
    def _():
        wb_ref[...] = w_ref[...].astype(BF16)

    rows = h_ref.shape[0] // PROJ_ROW_SLABS
    for m in range(PROJ_ROW_SLABS):
        sl = slice(m * rows, (m + 1) * rows)
        acc = jnp.dot(h_ref[sl, :], wb_ref[...], preferred_element_type=F32)
        if kind == "gelu":
            acc = jax.nn.gelu(acc)
        elif kind == "sigmoid":
            acc = jax.nn.sigmoid(acc)
        elif kind == "headnorm":
            acc = _head_norm(acc, extra[0][...])
        o_ref[sl, :] = acc.astype(o_ref.dtype)

    for src_ref, dst_ref in zip(cast_in, cast_out):
        dst_ref[...] = src_ref[...].astype(BF16)


def _proj(h, w, col0, n, kind, out_dtype, gain=None, casts=(), tm=1024, tn=1024):
    t, d = h.shape
    tm = _tile(t, tm)
    tn = _tile(n, tn)
    assert col0 % LANES == 0 and tn % LANES == 0 and tm % (PROJ_ROW_SLABS * 2 * SUBLANES) == 0
    n_i = t // tm
    n_steps = (n // tn) * n_i
    in_specs = [pl.BlockSpec((tm, d), lambda j, i: (i, 0)),
                pl.BlockSpec((pl.Element(d), pl.Element(tn)),
                             lambda j, i: (0, pl.multiple_of(col0 + j * tn, LANES)))]
    args = [h, w]
    if gain is not None:
        in_specs.append(pl.BlockSpec((1, tn), lambda j, i: (0, j)))
        args.append(gain)
    out_specs = [pl.BlockSpec((tm, tn), lambda j, i: (i, j))]
    out_shape = [jax.ShapeDtypeStruct((t, n), out_dtype)]
    for c in casts:
        rows = c.shape[0] // n_steps
        assert rows * n_steps == c.shape[0] and rows % (2 * SUBLANES) == 0, (c.shape, n_steps)
        spec = pl.BlockSpec((rows, c.shape[1]), lambda j, i: (j * n_i + i, 0))
        in_specs.append(spec)
        args.append(c)
        out_specs.append(spec)
        out_shape.append(jax.ShapeDtypeStruct(c.shape, BF16))
    return pl.pallas_call(
        functools.partial(_proj_kernel, kind=kind, n_cast=len(casts)),
        grid=(n // tn, n_i),
        in_specs=in_specs,
        out_specs=out_specs,
        out_shape=out_shape,
        scratch_shapes=[pltpu.VMEM((d, tn), BF16)],
        compiler_params=_cparams("parallel", "arbitrary"),
        name="proj_" + kind,
    )(*args)


def _gate_window_starts(d_rnn):
    blk = d_rnn // N_RNN_BLOCKS
    n_tiles = d_rnn // LANES
    starts = []
    for j in range(n_tiles):
        b0 = (LANES * j) // blk
        b1 = (LANES * j + LANES - 1) // blk
        s = min((blk * b0) // LANES, n_tiles - GATE_WINDOW // LANES)
        assert s * LANES <= blk * b0 and blk * (b1 + 1) <= s * LANES + GATE_WINDOW
        starts.append(s)
    return starts


def _band_gate_weights(w_a, w_x):
    nb, blk, _ = w_a.shape
    d_rnn = nb * blk
    starts = _gate_window_starts(d_rnn)

    def band(w):
        tiles = []
        for j, s in enumerate(starts):
            c_lo, c_hi = LANES * j, LANES * (j + 1)
            pieces = []
            for b in range(c_lo // blk, (c_hi - 1) // blk + 1):
                oc0, oc1 = max(c_lo, blk * b), min(c_hi, blk * (b + 1))
                r0 = blk * b - s * LANES
                pieces.append(jnp.pad(w[b, :, oc0 - blk * b:oc1 - blk * b],
                                      ((r0, GATE_WINDOW - blk - r0), (0, 0))))
            tiles.append(jnp.concatenate(pieces, axis=1))
        return jnp.stack(tiles)

    return jnp.concatenate([band(w_a), band(w_x)], axis=-1).astype(BF16)


def _rnn_kernel(xr_ref, gg_ref, cw_ref, cb_ref, wband_ref, ba_ref, bx_ref, lam_ref, y_ref,
                tail_ref, xin_ref, hout_ref, xc_ref, xcb_ref, hcar_ref, *, tc, d_rnn, starts):
    n_tiles = d_rnn // LANES
    seg = tc // SUBLANES
    pitch = seg + SUBLANES

    @pl.when(pl.program_id(1) == 0)
    def _():
        tail_ref[...] = jnp.zeros_like(tail_ref)
        hcar_ref[...] = jnp.zeros_like(hcar_ref)

    sub = lax.broadcasted_iota(jnp.int32, (SUBLANES, LANES), 0)

    def seg_rows(p):
        return slice(p * seg, (p + 1) * seg)

    def slab_rows(p):
        return slice(p * pitch, p * pitch + seg)

    for j in range(n_tiles):
        cols = slice(j * LANES, (j + 1) * LANES)
        for p in range(SUBLANES):
            xin_ref[j, slab_rows(p), :] = xr_ref[seg_rows(p), cols]
        x = [xin_ref[j, pl.ds(t, SUBLANES, stride=pitch), :] for t in range(seg)]

        def shifted(t, s):
            if t >= s:
                return x[t - s]
            i = SUBLANES - s + t
            prev = jnp.broadcast_to(tail_ref[i:i + 1, cols], (SUBLANES, LANES))
            return jnp.where(sub >= 1, pltpu.roll(x[seg + t - s], 1, axis=0), prev)

        xc = []
        for t in range(seg):
            acc = cb_ref[:, cols] + cw_ref[CONV_WIDTH - 1:CONV_WIDTH, cols] * x[t]
            for s in range(1, CONV_WIDTH):
                k = CONV_WIDTH - 1 - s
                acc = acc + cw_ref[k:k + 1, cols] * shifted(t, s)
            xc.append(acc)
        xc = jnp.concatenate(xc, axis=0)
        xc_ref[:, cols] = xc
        xcb_ref[:, cols] = xc.astype(BF16)

    tail_ref[...] = xr_ref[tc - SUBLANES:tc, :]

    for j in range(n_tiles):
        cols = slice(j * LANES, (j + 1) * LANES)
        ks = starts[j] * LANES
        pre = jnp.dot(xcb_ref[:, ks:ks + GATE_WINDOW], wband_ref[j], preferred_element_type=F32)
        r = 0.5 * jnp.tanh(0.5 * (pre[:, :LANES] + ba_ref[:, cols])) + 0.5
        i = 0.5 * jnp.tanh(0.5 * (pre[:, LANES:] + bx_ref[:, cols])) + 0.5
        nlam = -lam_ref[:, cols]
        softplus = jnp.maximum(nlam, 0.0) + jnp.log1p(jnp.exp(-jnp.abs(nlam)))
        log_a = r * ((-LRU_C) * softplus)
        a = jnp.exp(log_a)
        y = -jnp.tanh(log_a) * (a * a + 1.0)
        u = y * lax.rsqrt(jnp.maximum(y, F32_TINY)) * (i * xc_ref[:, cols])
        a = a.reshape(seg, SUBLANES, LANES)
        u = u.reshape(seg, SUBLANES, LANES)

        h = u[0]
        decay = a[0]
        for t in range(1, seg):
            h = a[t] * h + u[t]
            decay = decay * a[t]
        for d in (1, 2, 4):
            m = sub >= d
            h = h + decay * jnp.where(m, pltpu.roll(h, d, axis=0), 0.0)
            decay = decay * jnp.where(m, pltpu.roll(decay, d, axis=0), 1.0)
        h_in = hcar_ref[:, cols]
        h_end = h + decay * h_in
        hcar_ref[:, cols] = jnp.broadcast_to(h_end[SUBLANES - 1:SUBLANES, :], (SUBLANES, LANES))
        h = jnp.where(sub >= 1, pltpu.roll(h_end, 1, axis=0), h_in)
        for t in range(seg):
            h = a[t] * h + u[t]
            hout_ref[j, pl.ds(t, SUBLANES, stride=pitch), :] = h
        for p in range(SUBLANES):
            y_ref[seg_rows(p), cols] = (gg_ref[seg_rows(p), cols] * hout_ref[j, slab_rows(p), :]).astype(y_ref.dtype)


def _rnn(xr, gg, conv_w, conv_b, wband, b_a, b_x, lam, tc=256):
    b, s, d_rnn = xr.shape
    tc = _tile(s, tc)
    starts = _gate_window_starts(d_rnn)
    n_tiles = d_rnn // LANES
    slab = (n_tiles, tc + SUBLANES * SUBLANES, LANES)
    row = lambda v: v.reshape(1, d_rnn)
    full2 = lambda shape: pl.BlockSpec(shape, lambda bi, ci: (0, 0))
    chunk = pl.BlockSpec((None, tc, d_rnn), lambda bi, ci: (bi, ci, 0))
    return pl.pallas_call(
        functools.partial(_rnn_kernel, tc=tc, d_rnn=d_rnn, starts=starts),
        grid=(b, s // tc),
        in_specs=[chunk, chunk,
                  full2((CONV_WIDTH, d_rnn)), full2((1, d_rnn)),
                  pl.BlockSpec(wband.shape, lambda bi, ci: (0, 0, 0)),
                  full2((1, d_rnn)), full2((1, d_rnn)), full2((1, d_rnn))],
        out_specs=chunk,
        out_shape=jax.ShapeDtypeStruct((b, s, d_rnn), BF16),
        scratch_shapes=[pltpu.VMEM((SUBLANES, d_rnn), F32),
                        pltpu.VMEM(slab, F32),
                        pltpu.VMEM(slab, F32),
                        pltpu.VMEM((tc, d_rnn), F32),
                        pltpu.VMEM((tc, d_rnn), BF16),
                        pltpu.VMEM((SUBLANES, d_rnn), F32)],
        compiler_params=_cparams("parallel", "arbitrary"),
        name="rnn",
    )(xr, gg, conv_w, row(conv_b), wband, row(b_a), row(b_x), row(lam))


PRUNE_LOG = -104.0
SOFTPLUS_LINEAR = 80.0


def _attn_kernel(q_ref, k_ref, v_ref, tri_ref, o_ref, *, tq, hp):
    qi = pl.program_id(2)
    row = lax.broadcasted_iota(jnp.int32, (tq, tq), 0)
    col = lax.broadcasted_iota(jnp.int32, (tq, tq), 1)
    causal = col < row

    def block(hd, kb, run, acc, diag):
        lanes = slice(hd * HEAD_DIM, (hd + 1) * HEAD_DIM)
        start = pl.multiple_of(kb * tq, tq)
        k = k_ref[pl.ds(start, tq), lanes]
        v = v_ref[pl.ds(start, tq), lanes]
        z = lax.dot_general(q_ref[:, lanes], k, (((1,), (1,)), ((), ())), preferred_element_type=F32)
        sp = jnp.where(z > SOFTPLUS_LINEAR, z, jnp.log(1.0 + jnp.exp(z)))
        if diag:
            sp = jnp.where(causal, sp, 0.0)
        excl = jnp.dot(sp.astype(BF16), tri_ref[...], preferred_element_type=F32)
        w = jnp.exp((z - sp) + excl + run)
        if diag:
            w = jnp.where(causal, w, 0.0)
        acc = acc + jnp.dot(w.astype(BF16), v, preferred_element_type=F32)
        run = run - jnp.sum(sp, axis=-1, keepdims=True)
        return run, acc

    def run_max(state):
        m = jnp.max(state[0][0])
        for run, _ in state[1:]:
            m = jnp.maximum(m, jnp.max(run))
        return m

    zero = (jnp.zeros((tq, 1), F32), jnp.zeros((tq, HEAD_DIM), F32))

    def store(state):
        for hd, (_, acc) in enumerate(state):
            o_ref[:, hd * HEAD_DIM:(hd + 1) * HEAD_DIM] = acc.astype(o_ref.dtype)

    @pl.when(qi == 0)
    def _():
        store(tuple(block(hd, qi, *zero, True) for hd in range(hp)))

    @pl.when(qi > 0)
    def _():
        state = tuple(block(hd, qi - 1, *block(hd, qi, *zero, True), False) for hd in range(hp))

        def cond(c):
            kb, rmax, _ = c
            return jnp.logical_and(kb >= 0, rmax > PRUNE_LOG)

        def body(c):
            kb, _, st = c
            st = tuple(block(hd, kb, run, acc, False) for hd, (run, acc) in enumerate(st))
            return kb - 1, run_max(st), st

        _, _, state = lax.while_loop(cond, body, (qi - 2, run_max(state), state))
        store(state)


def _attention(qk, v, tq=256, hp=4):
    b, s, _ = v.shape
    tq = _tile(s, tq)
    hw = hp * HEAD_DIM
    n_hg = N_HEADS // hp
    tri = jnp.asarray(-np.tril(np.ones((tq, tq), np.float32), -1), BF16)
    return pl.pallas_call(
        functools.partial(_attn_kernel, tq=tq, hp=hp),
        grid=(b, n_hg, s // tq),
        in_specs=[pl.BlockSpec((None, tq, hw), lambda bi, h, qi: (bi, qi, h)),
                  pl.BlockSpec((None, s, hw), lambda bi, h, qi: (bi, 0, n_hg + h)),
                  pl.BlockSpec((None, s, hw), lambda bi, h, qi: (bi, 0, h)),
                  pl.BlockSpec((tq, tq), lambda bi, h, qi: (0, 0))],
        out_specs=pl.BlockSpec((None, tq, hw), lambda bi, h, qi: (bi, qi, h)),
        out_shape=jax.ShapeDtypeStruct(v.shape, BF16),
        compiler_params=_cparams("parallel", "parallel", "arbitrary"),
        name="attention",
    )(qk, qk, v, tri)


def _mix_kernel(yr_ref, oa_ref, wr_ref, wa_ref, sr_ref, sa_ref, o_ref):
    ya = jnp.dot(yr_ref[...], wr_ref[...], preferred_element_type=F32)
    yb = jnp.dot(oa_ref[...], wa_ref[...], preferred_element_type=F32)
    o_ref[...] = (sr_ref[...] * ya + sa_ref[...] * yb).astype(o_ref.dtype)


def _mix(yr, oa, w_r, w_a, gates, tm=1024, tn=1024):
    t, d_rnn = yr.shape
    d_attn = oa.shape[1]
    n = w_r.shape[1]
    tm = _tile(t, tm)
    tn = _tile(n, tn)
    nj = n // tn
    return pl.pallas_call(
        _mix_kernel,
        grid=(nj, t // tm),
        in_specs=[pl.BlockSpec((tm, d_rnn), lambda j, i: (i, 0)),
                  pl.BlockSpec((tm, d_attn), lambda j, i: (i, 0)),
                  pl.BlockSpec((d_rnn, tn), lambda j, i: (0, j), pipeline_mode=pl.Buffered(1)),
                  pl.BlockSpec((d_attn, tn), lambda j, i: (0, j), pipeline_mode=pl.Buffered(1)),
                  pl.BlockSpec((tm, tn), lambda j, i: (i, j)),
                  pl.BlockSpec((tm, tn), lambda j, i: (i, nj + j))],
        out_specs=pl.BlockSpec((tm, tn), lambda j, i: (i, j)),
        out_shape=jax.ShapeDtypeStruct((t, n), BF16),
        compiler_params=_cparams("parallel", "parallel"),
        name="mix",
    )(yr, oa, w_r, w_a, gates, gates)


def _oproj_kernel(m_ref, w_ref, x_ref, g_ref, x1_ref, h_ref):
    x1 = x_ref[...] + jnp.dot(m_ref[...], w_ref[...], preferred_element_type=F32)
    x1_ref[...] = x1
    ms = jnp.mean(x1 * x1, axis=-1, keepdims=True)
    h_ref[...] = (x1 * lax.rsqrt(ms + EPS) * g_ref[...]).astype(h_ref.dtype)


def _oproj(mix, w_o, x, g_next, tm=512):
    t, d = x.shape
    tm = _tile(t, tm)
    return pl.pallas_call(
        _oproj_kernel,
        grid=(t // tm,),
        in_specs=[pl.BlockSpec((tm, d), lambda i: (i, 0)),
                  pl.BlockSpec((d, d), lambda i: (0, 0)),
                  pl.BlockSpec((tm, d), lambda i: (i, 0)),
                  pl.BlockSpec((1, d), lambda i: (0, 0))],
        out_specs=[pl.BlockSpec((tm, d), lambda i: (i, 0)),
                   pl.BlockSpec((tm, d), lambda i: (i, 0))],
        out_shape=[jax.ShapeDtypeStruct((t, d), F32), jax.ShapeDtypeStruct((t, d), BF16)],
        compiler_params=_cparams("parallel"),
        name="oproj",
    )(mix, w_o, x, g_next.reshape(1, d))


def _ffn_kernel(h_ref, wg_ref, wu_ref, wd_ref, x_ref, g_ref, x2_ref, h3_ref, acc_ref):
    f = pl.program_id(1)

    @pl.when(f == 0)
    def _():
        acc_ref[...] = x_ref[...]

    h = h_ref[...]
    g = jnp.dot(h, wg_ref[...], preferred_element_type=F32)
    u = jnp.dot(h, wu_ref[...], preferred_element_type=F32)
    act = (jax.nn.silu(g) * u).astype(BF16)
    acc_ref[...] += jnp.dot(act, wd_ref[...], preferred_element_type=F32)

    @pl.when(f == pl.num_programs(1) - 1)
    def _():
        x2 = acc_ref[...]
        x2_ref[...] = x2
        ms = jnp.mean(x2 * x2, axis=-1, keepdims=True)
        h3_ref[...] = (x2 * lax.rsqrt(ms + EPS) * g_ref[...]).astype(h3_ref.dtype)


def _ffn(h2, w_gu, w_d, x1, g_next, tm=512, tf=512):
    t, d = x1.shape
    d_ff = w_d.shape[0]
    tm = _tile(t, tm)
    tf = _tile(d_ff, tf)
    nf = d_ff // tf
    return pl.pallas_call(
        _ffn_kernel,
        grid=(t // tm, d_ff // tf),
        in_specs=[pl.BlockSpec((tm, d), lambda i, f: (i, 0)),
                  pl.BlockSpec((d, tf), lambda i, f: (0, f)),
                  pl.BlockSpec((d, tf), lambda i, f: (0, nf + f)),
                  pl.BlockSpec((tf, d), lambda i, f: (f, 0)),
                  pl.BlockSpec((tm, d), lambda i, f: (i, 0)),
                  pl.BlockSpec((1, d), lambda i, f: (0, 0))],
        out_specs=[pl.BlockSpec((tm, d), lambda i, f: (i, 0)),
                   pl.BlockSpec((tm, d), lambda i, f: (i, 0))],
        out_shape=[jax.ShapeDtypeStruct((t, d), F32), jax.ShapeDtypeStruct((t, d), BF16)],
        scratch_shapes=[pltpu.VMEM((tm, d), F32)],
        compiler_params=_cparams("parallel", "arbitrary"),
        name="ffn",
    )(h2, w_gu, w_gu, w_d, x1, g_next.reshape(1, d))


def _ple_kernel(h_ref, wg_ref, p_ref, wp_ref, x_ref, o_ref):
    gate = jax.nn.sigmoid(jnp.dot(h_ref[...], wg_ref[...], preferred_element_type=F32))
    emb = jnp.dot(p_ref[...], wp_ref[...], preferred_element_type=F32)
    o_ref[...] = x_ref[...] + gate * emb


def _ple(h3, w_gate, p, w_proj, x2, tm=1024, tn=1024):
    t, d = x2.shape
    dp = p.shape[1]
    tm = _tile(t, tm)
    tn = _tile(d, tn)
    return pl.pallas_call(
        _ple_kernel,
        grid=(t // tm, d // tn),
        in_specs=[pl.BlockSpec((tm, d), lambda i, j: (i, 0)),
                  pl.BlockSpec((d, tn), lambda i, j: (0, j)),
                  pl.BlockSpec((tm, dp), lambda i, j: (i, 0)),
                  pl.BlockSpec((dp, tn), lambda i, j: (0, j)),
                  pl.BlockSpec((tm, tn), lambda i, j: (i, j))],
        out_specs=pl.BlockSpec((tm, tn), lambda i, j: (i, j)),
        out_shape=jax.ShapeDtypeStruct((t, d), F32),
        compiler_params=_cparams("parallel", "parallel"),
        name="ple",
    )(h3, w_gate, p, w_proj, x2)


def _layer(x, p, g_mix, w_in, conv_w, conv_b, w_rg_a, b_rg_a, w_rg_x, b_rg_x, lru_lambda, q_gain,
           k_gain, w_rnn_out, w_attn_out, w_o, g_ffn, w_ffn_gu, w_ffn_down, g_ple, w_ple_gate,
           w_ple_proj):
    b, s, d = x.shape
    t = b * s
    d_rnn = w_rnn_out.shape[0]
    d_attn = w_attn_out.shape[0]
    x2d = x.reshape(t, d)

    o0 = 0
    o1 = o0 + d_rnn
    o2 = o1 + d_rnn
    o3 = o2 + 2 * d_attn
    o4 = o3 + d_attn
    o5 = o4 + 2 * d
    assert o5 == w_in.shape[1]

    h = _rmsnorm(x2d, g_mix)
    xr, = _proj(h, w_in, o0, d_rnn, "none", F32, tn=d_rnn // 2)
    gg, = _proj(h, w_in, o1, d_rnn, "gelu", F32, tn=d_rnn // 2)
    qk_gain = jnp.concatenate([jnp.tile(q_gain * (1.0 / math.sqrt(HEAD_DIM)), N_HEADS),
                               jnp.tile(k_gain, N_HEADS)]).reshape(1, 2 * d_attn)
    qk, w_gu_b, w_attn_out_b = _proj(h, w_in, o2, 2 * d_attn, "headnorm", BF16, gain=qk_gain,
                                     casts=(w_ffn_gu, w_attn_out))
    v, w_rnn_out_b = _proj(h, w_in, o3, d_attn, "none", BF16, casts=(w_rnn_out,))
    gates, w_down_b, w_o_b, w_ple_gate_b = _proj(h, w_in, o4, 2 * d, "sigmoid", BF16,
                                                 casts=(w_ffn_down, w_o, w_ple_gate))

    wband = _band_gate_weights(w_rg_a, w_rg_x)
    yr = _rnn(xr.reshape(b, s, d_rnn), gg.reshape(b, s, d_rnn), conv_w, conv_b, wband,
              b_rg_a, b_rg_x, lru_lambda)
    oa = _attention(qk.reshape(b, s, 2 * d_attn), v.reshape(b, s, d_attn))

    mix = _mix(yr.reshape(t, d_rnn), oa.reshape(t, d_attn), w_rnn_out_b, w_attn_out_b, gates)
    x1, h2 = _oproj(mix, w_o_b, x2d, g_ffn)
    x2, h3 = _ffn(h2, w_gu_b, w_down_b, x1, g_ple)
    out = _ple(h3, w_ple_gate_b, p.reshape(t, -1).astype(BF16), w_ple_proj.astype(BF16), x2)
    return out.reshape(b, s, d)


def kernel(x, p, g_mix, w_in, conv_w, conv_b, w_rg_a, b_rg_a, w_rg_x, b_rg_x, lru_lambda, q_gain,
           k_gain, w_rnn_out, w_attn_out, w_o, g_ffn, w_ffn_gu, w_ffn_down, g_ple, w_ple_gate,
           w_ple_proj):
    params = (g_mix, w_in, conv_w, conv_b, w_rg_a, b_rg_a, w_rg_x, b_rg_x, lru_lambda, q_gain,
              k_gain, w_rnn_out, w_attn_out, w_o, g_ffn, w_ffn_gu, w_ffn_down, g_ple, w_ple_gate,
              w_ple_proj)
    for i in range(p.shape[0]):
        x = _layer(x, p[i], *[a[i] for a in params])
    return x
```

```python
import functools
import math

import jax
import jax.numpy as jnp
import numpy as np
from jax import lax
from jax.experimental import pallas as pl
from jax.experimental.pallas import tpu as pltpu

F32 = jnp.float32
BF16 = jnp.bfloat16

EPS = 1e-6
LRU_C = 8.0
N_HEADS = 16
HEAD_DIM = 128
N_RNN_BLOCKS = 16
CONV_WIDTH = 4

LANES = 128
SUBLANES = 8
GATE_WINDOW = 4 * LANES
VMEM_LIMIT = 56 * 1024 * 1024
F32_TINY = float(np.finfo(np.float32).tiny)


def _cparams(*sem):
    return pltpu.CompilerParams(dimension_semantics=sem, vmem_limit_bytes=VMEM_LIMIT)


def _tile(n, pref):
    t = min(n, pref)
    assert n % t == 0, (n, t)
    return t


def _rmsnorm_kernel(x_ref, g_ref, o_ref):
    x = x_ref[...]
    ms = jnp.mean(x * x, axis=-1, keepdims=True)
    o_ref[...] = (x * lax.rsqrt(ms + EPS) * g_ref[...]).astype(o_ref.dtype)


def _rmsnorm(x, g, tm=512):
    t, d = x.shape
    tm = _tile(t, tm)
    return pl.pallas_call(
        _rmsnorm_kernel,
        grid=(t // tm,),
        in_specs=[pl.BlockSpec((tm, d), lambda i: (i, 0)),
                  pl.BlockSpec((1, d), lambda i: (0, 0))],
        out_specs=pl.BlockSpec((tm, d), lambda i: (i, 0)),
        out_shape=jax.ShapeDtypeStruct((t, d), BF16),
        compiler_params=_cparams("parallel"),
        name="rmsnorm",
    )(x, g.reshape(1, d))


def _head_norm(acc, gain):
    outs = []
    for c in range(acc.shape[1] // HEAD_DIM):
        blk = acc[:, c * HEAD_DIM:(c + 1) * HEAD_DIM]
        ms = jnp.mean(blk * blk, axis=-1, keepdims=True)
        outs.append(blk * lax.rsqrt(ms + EPS) * gain[:, c * HEAD_DIM:(c + 1) * HEAD_DIM])
    return jnp.concatenate(outs, axis=1)


PROJ_ROW_SLABS = 4


def _proj_kernel(h_ref, w_ref, *rest, kind, n_cast):
    wb_ref = rest[-1]
    cast_out = rest[len(rest) - 1 - n_cast:-1]
    o_ref = rest[-2 - n_cast]
    cast_in = rest[-2 - 2 * n_cast:-2 - n_cast]
    extra = rest[:-2 - 2 * n_cast]

    @pl.when(pl.program_id(1) == 0)
    def _():
        wb_ref[...] = w_ref[...].astype(BF16)

    rows = h_ref.shape[0] // PROJ_ROW_SLABS
    for m in range(PROJ_ROW_SLABS):
        sl = slice(m * rows, (m + 1) * rows)
        acc = jnp.dot(h_ref[sl, :], wb_ref[...], preferred_element_type=F32)
        if kind == "gelu":
            acc = jax.nn.gelu(acc)
        elif kind == "sigmoid":
            acc = jax.nn.sigmoid(acc)
        elif kind == "headnorm":
            acc = _head_norm(acc, extra[0][...])
        o_ref[sl, :] = acc.astype(o_ref.dtype)

    for src_ref, dst_ref in zip(cast_in, cast_out):
        dst_ref[...] = src_ref[...].astype(BF16)


def _proj(h, w, col0, n, kind, out_dtype, gain=None, casts=(), tm=1024, tn=1024):
    t, d = h.shape
    tm = _tile(t, tm)
    tn = _tile(n, tn)
    assert col0 % LANES == 0 and tn % LANES == 0 and tm % (PROJ_ROW_SLABS * 2 * SUBLANES) == 0
    n_i = t // tm
    n_steps = (n // tn) * n_i
    in_specs = [pl.BlockSpec((tm, d), lambda j, i: (i, 0)),
                pl.BlockSpec((pl.Element(d), pl.Element(tn)),
                             lambda j, i: (0, pl.multiple_of(col0 + j * tn, LANES)))]
    args = [h, w]
    if gain is not None:
        in_specs.append(pl.BlockSpec((1, tn), lambda j, i: (0, j)))
        args.append(gain)
    out_specs = [pl.BlockSpec((tm, tn), lambda j, i: (i, j))]
    out_shape = [jax.ShapeDtypeStruct((t, n), out_dtype)]
    for c in casts:
        rows = c.shape[0] // n_steps
        assert rows * n_steps == c.shape[0] and rows % (2 * SUBLANES) == 0, (c.shape, n_steps)
        spec = pl.BlockSpec((rows, c.shape[1]), lambda j, i: (j * n_i + i, 0))
        in_specs.append(spec)
        args.append(c)
        out_specs.append(spec)
        out_shape.append(jax.ShapeDtypeStruct(c.shape, BF16))
    return pl.pallas_call(
        functools.partial(_proj_kernel, kind=kind, n_cast=len(casts)),
        grid=(n // tn, n_i),
        in_specs=in_specs,
        out_specs=out_specs,
        out_shape=out_shape,
        scratch_shapes=[pltpu.VMEM((d, tn), BF16)],
        compiler_params=_cparams("parallel", "arbitrary"),
        name="proj_" + kind,
    )(*args)


def _gate_window_starts(d_rnn):
    blk = d_rnn // N_RNN_BLOCKS
    n_tiles = d_rnn // LANES
    starts = []
    for j in range(n_tiles):
        b0 = (LANES * j) // blk
        b1 = (LANES * j + LANES - 1) // blk
        s = min((blk * b0) // LANES, n_tiles - GATE_WINDOW // LANES)
        assert s * LANES <= blk * b0 and blk * (b1 + 1) <= s * LANES + GATE_WINDOW
        starts.append(s)
    return starts


def _band_gate_weights(w_a, w_x):
    nb, blk, _ = w_a.shape
    d_rnn = nb * blk
    starts = _gate_window_starts(d_rnn)

    def band(w):
        tiles = []
        for j, s in enumerate(starts):
            c_lo, c_hi = LANES * j, LANES * (j + 1)
            pieces = []
            for b in range(c_lo // blk, (c_hi - 1) // blk + 1):
                oc0, oc1 = max(c_lo, blk * b), min(c_hi, blk * (b + 1))
                r0 = blk * b - s * LANES
                pieces.append(jnp.pad(w[b, :, oc0 - blk * b:oc1 - blk * b],
                                      ((r0, GATE_WINDOW - blk - r0), (0, 0))))
            tiles.append(jnp.concatenate(pieces, axis=1))
        return jnp.stack(tiles)

    return jnp.concatenate([band(w_a), band(w_x)], axis=-1).astype(BF16)


def _rnn_kernel(xr_ref, gg_ref, cw_ref, cb_ref, wband_ref, ba_ref, bx_ref, lam_ref, y_ref,
                tail_ref, xin_ref, hout_ref, xc_ref, xcb_ref, hcar_ref, *, tc, d_rnn, starts):
    n_tiles = d_rnn // LANES
    seg = tc // SUBLANES
    pitch = seg + SUBLANES

    @pl.when(pl.program_id(1) == 0)
    def _():
        tail_ref[...] = jnp.zeros_like(tail_ref)
        hcar_ref[...] = jnp.zeros_like(hcar_ref)

    sub = lax.broadcasted_iota(jnp.int32, (SUBLANES, LANES), 0)

    def seg_rows(p):
        return slice(p * seg, (p + 1) * seg)

    def slab_rows(p):
        return slice(p * pitch, p * pitch + seg)

    for j in range(n_tiles):
        cols = slice(j * LANES, (j + 1) * LANES)
        for p in range(SUBLANES):
            xin_ref[j, slab_rows(p), :] = xr_ref[seg_rows(p), cols]
        x = [xin_ref[j, pl.ds(t, SUBLANES, stride=pitch), :] for t in range(seg)]

        def shifted(t, s):
            if t >= s:
                return x[t - s]
            i = SUBLANES - s + t
            prev = jnp.broadcast_to(tail_ref[i:i + 1, cols], (SUBLANES, LANES))
            return jnp.where(sub >= 1, pltpu.roll(x[seg + t - s], 1, axis=0), prev)

        xc = []
        for t in range(seg):
            acc = cb_ref[:, cols] + cw_ref[CONV_WIDTH - 1:CONV_WIDTH, cols] * x[t]
            for s in range(1, CONV_WIDTH):
                k = CONV_WIDTH - 1 - s
                acc = acc + cw_ref[k:k + 1, cols] * shifted(t, s)
            xc.append(acc)
        xc = jnp.concatenate(xc, axis=0)
        xc_ref[:, cols] = xc
        xcb_ref[:, cols] = xc.astype(BF16)

    tail_ref[...] = xr_ref[tc - SUBLANES:tc, :]

    for j in range(n_tiles):
        cols = slice(j * LANES, (j + 1) * LANES)
        ks = starts[j] * LANES
        pre = jnp.dot(xcb_ref[:, ks:ks + GATE_WINDOW], wband_ref[j], preferred_element_type=F32)
        r = 0.5 * jnp.tanh(0.5 * (pre[:, :LANES] + ba_ref[:, cols])) + 0.5
        i = 0.5 * jnp.tanh(0.5 * (pre[:, LANES:] + bx_ref[:, cols])) + 0.5
        nlam = -lam_ref[:, cols]
        softplus = jnp.maximum(nlam, 0.0) + jnp.log1p(jnp.exp(-jnp.abs(nlam)))
        log_a = r * ((-LRU_C) * softplus)
        a = jnp.exp(log_a)
        y = -jnp.tanh(log_a) * (a * a + 1.0)
        u = y * lax.rsqrt(jnp.maximum(y, F32_TINY)) * (i * xc_ref[:, cols])
        a = a.reshape(seg, SUBLANES, LANES)
        u = u.reshape(seg, SUBLANES, LANES)

        h = u[0]
        decay = a[0]
        for t in range(1, seg):
            h = a[t] * h + u[t]
            decay = decay * a[t]
        for d in (1, 2, 4):
            m = sub >= d
            h = h + decay * jnp.where(m, pltpu.roll(h, d, axis=0), 0.0)
            decay = decay * jnp.where(m, pltpu.roll(decay, d, axis=0), 1.0)
        h_in = hcar_ref[:, cols]
        h_end = h + decay * h_in
        hcar_ref[:, cols] = jnp.broadcast_to(h_end[SUBLANES - 1:SUBLANES, :], (SUBLANES, LANES))
        h = jnp.where(sub >= 1, pltpu.roll(h_end, 1, axis=0), h_in)
        for t in range(seg):
            h = a[t] * h + u[t]
            hout_ref[j, pl.ds(t, SUBLANES, stride=pitch), :] = h
        for p in range(SUBLANES):
            y_ref[seg_rows(p), cols] = (gg_ref[seg_rows(p), cols] * hout_ref[j, slab_rows(p), :]).astype(y_ref.dtype)


def _rnn(xr, gg, conv_w, conv_b, wband, b_a, b_x, lam, tc=256):
    b, s, d_rnn = xr.shape
    tc = _tile(s, tc)
    starts = _gate_window_starts(d_rnn)
    n_tiles = d_rnn // LANES
    slab = (n_tiles, tc + SUBLANES * SUBLANES, LANES)
    row = lambda v: v.reshape(1, d_rnn)
    full2 = lambda shape: pl.BlockSpec(shape, lambda bi, ci: (0, 0))
    chunk = pl.BlockSpec((None, tc, d_rnn), lambda bi, ci: (bi, ci, 0))
    return pl.pallas_call(
        functools.partial(_rnn_kernel, tc=tc, d_rnn=d_rnn, starts=starts),
        grid=(b, s // tc),
        in_specs=[chunk, chunk,
                  full2((CONV_WIDTH, d_rnn)), full2((1, d_rnn)),
                  pl.BlockSpec(wband.shape, lambda bi, ci: (0, 0, 0)),
                  full2((1, d_rnn)), full2((1, d_rnn)), full2((1, d_rnn))],
        out_specs=chunk,
        out_shape=jax.ShapeDtypeStruct((b, s, d_rnn), BF16),
        scratch_shapes=[pltpu.VMEM((SUBLANES, d_rnn), F32),
                        pltpu.VMEM(slab, F32),
                        pltpu.VMEM(slab, F32),
                        pltpu.VMEM((tc, d_rnn), F32),
                        pltpu.VMEM((tc, d_rnn), BF16),
                        pltpu.VMEM((SUBLANES, d_rnn), F32)],
        compiler_params=_cparams("parallel", "arbitrary"),
        name="rnn",
    )(xr, gg, conv_w, row(conv_b), wband, row(b_a), row(b_x), row(lam))


PRUNE_LOG = -104.0
SOFTPLUS_LINEAR = 80.0


def _attn_kernel(q_ref, k_ref, v_ref, tri_ref, o_ref, *, tq, hp):
    qi = pl.program_id(2)
    row = lax.broadcasted_iota(jnp.int32, (tq, tq), 0)
    col = lax.broadcasted_iota(jnp.int32, (tq, tq), 1)
    causal = col < row

    def sweep(last_kb, n_blocks, diag, state):
        rows = pl.ds(pl.multiple_of((last_kb - (n_blocks - 1)) * tq, tq), n_blocks * tq)
        zs, sps = [], []
        for hd in range(hp):
            lanes = slice(hd * HEAD_DIM, (hd + 1) * HEAD_DIM)
            z = lax.dot_general(q_ref[:, lanes], k_ref[rows, lanes], (((1,), (1,)), ((), ())),
                                preferred_element_type=F32)
            for blk in range(n_blocks):
                zb = z[:, blk * tq:(blk + 1) * tq]
                sp = jnp.where(zb > SOFTPLUS_LINEAR, zb, jnp.log(1.0 + jnp.exp(zb)))
                if diag and blk == n_blocks - 1:
                    sp = jnp.where(causal, sp, 0.0)
                zs.append(zb)
                sps.append(sp)
        excl = jnp.dot(jnp.concatenate([sp.astype(BF16) for sp in sps], axis=0), tri_ref[...],
                       preferred_element_type=F32)
        out = []
        for hd in range(hp):
            lanes = slice(hd * HEAD_DIM, (hd + 1) * HEAD_DIM)
            run, acc = state[hd]
            ws = [None] * n_blocks
            for blk in reversed(range(n_blocks)):
                u = hd * n_blocks + blk
                w = jnp.exp((zs[u] - sps[u]) + excl[u * tq:(u + 1) * tq] + run)
                if diag and blk == n_blocks - 1:
                    w = jnp.where(causal, w, 0.0)
                ws[blk] = w.astype(BF16)
                run = run - jnp.sum(sps[u], axis=-1, keepdims=True)
            acc = acc + jnp.dot(jnp.concatenate(ws, axis=1), v_ref[rows, lanes], preferred_element_type=F32)
            out.append((run, acc))
        return tuple(out)

    def run_max(state):
        m = jnp.max(state[0][0])
        for run, _ in state[1:]:
            m = jnp.maximum(m, jnp.max(run))
        return m

    zero = tuple((jnp.zeros((tq, 1), F32), jnp.zeros((tq, HEAD_DIM), F32)) for _ in range(hp))

    def store(state):
        for hd, (_, acc) in enumerate(state):
            o_ref[:, hd * HEAD_DIM:(hd + 1) * HEAD_DIM] = acc.astype(o_ref.dtype)

    @pl.when(qi == 0)
    def _():
        store(sweep(qi, 1, True, zero))

    @pl.when(qi > 0)
    def _():
        state = sweep(qi, 2, True, zero)

        def cond(c):
            kb, rmax, _ = c
            return jnp.logical_and(kb >= 0, rmax > PRUNE_LOG)

        def body(c):
            kb, _, st = c
            st = sweep(kb, 1, False, st)
            return kb - 1, run_max(st), st

        _, _, state = lax.while_loop(cond, body, (qi - 2, run_max(state), state))
        store(state)


def _attention(qk, v, tq=256, hp=8):
    b, s, _ = v.shape
    tq = _tile(s, tq)
    hw = hp * HEAD_DIM
    n_hg = N_HEADS // hp
    tri = jnp.asarray(-np.tril(np.ones((tq, tq), np.float32), -1), BF16)
    return pl.pallas_call(
        functools.partial(_attn_kernel, tq=tq, hp=hp),
        grid=(b, n_hg, s // tq),
        in_specs=[pl.BlockSpec((None, tq, hw), lambda bi, h, qi: (bi, qi, h)),
                  pl.BlockSpec((None, s, hw), lambda bi, h, qi: (bi, 0, n_hg + h)),
                  pl.BlockSpec((None, s, hw), lambda bi, h, qi: (bi, 0, h)),
                  pl.BlockSpec((tq, tq), lambda bi, h, qi: (0, 0))],
        out_specs=pl.BlockSpec((None, tq, hw), lambda bi, h, qi: (bi, qi, h)),
        out_shape=jax.ShapeDtypeStruct(v.shape, BF16),
        compiler_params=_cparams("parallel", "parallel", "arbitrary"),
        name="attention",
    )(qk, qk, v, tri)


def _mix_kernel(yr_ref, oa_ref, wr_ref, wa_ref, sr_ref, sa_ref, o_ref):
    ya = jnp.dot(yr_ref[...], wr_ref[...], preferred_element_type=F32)
    yb = jnp.dot(oa_ref[...], wa_ref[...], preferred_element_type=F32)
    o_ref[...] = (sr_ref[...] * ya + sa_ref[...] * yb).astype(o_ref.dtype)


def _mix(yr, oa, w_r, w_a, gates, tm=1024, tn=1024):
    t, d_rnn = yr.shape
    d_attn = oa.shape[1]
    n = w_r.shape[1]
    tm = _tile(t, tm)
    tn = _tile(n, tn)
    nj = n // tn
    return pl.pallas_call(
        _mix_kernel,
        grid=(nj, t // tm),
        in_specs=[pl.BlockSpec((tm, d_rnn), lambda j, i: (i, 0)),
                  pl.BlockSpec((tm, d_attn), lambda j, i: (i, 0)),
                  pl.BlockSpec((d_rnn, tn), lambda j, i: (0, j), pipeline_mode=pl.Buffered(1)),
                  pl.BlockSpec((d_attn, tn), lambda j, i: (0, j), pipeline_mode=pl.Buffered(1)),
                  pl.BlockSpec((tm, tn), lambda j, i: (i, j)),
                  pl.BlockSpec((tm, tn), lambda j, i: (i, nj + j))],
        out_specs=pl.BlockSpec((tm, tn), lambda j, i: (i, j)),
        out_shape=jax.ShapeDtypeStruct((t, n), BF16),
        compiler_params=_cparams("parallel", "parallel"),
        name="mix",
    )(yr, oa, w_r, w_a, gates, gates)


def _oproj_kernel(m_ref, w_ref, x_ref, g_ref, x1_ref, h_ref):
    x1 = x_ref[...] + jnp.dot(m_ref[...], w_ref[...], preferred_element_type=F32)
    x1_ref[...] = x1
    ms = jnp.mean(x1 * x1, axis=-1, keepdims=True)
    h_ref[...] = (x1 * lax.rsqrt(ms + EPS) * g_ref[...]).astype(h_ref.dtype)


def _oproj(mix, w_o, x, g_next, tm=512):
    t, d = x.shape
    tm = _tile(t, tm)
    return pl.pallas_call(
        _oproj_kernel,
        grid=(t // tm,),
        in_specs=[pl.BlockSpec((tm, d), lambda i: (i, 0)),
                  pl.BlockSpec((d, d), lambda i: (0, 0)),
                  pl.BlockSpec((tm, d), lambda i: (i, 0)),
                  pl.BlockSpec((1, d), lambda i: (0, 0))],
        out_specs=[pl.BlockSpec((tm, d), lambda i: (i, 0)),
                   pl.BlockSpec((tm, d), lambda i: (i, 0))],
        out_shape=[jax.ShapeDtypeStruct((t, d), F32), jax.ShapeDtypeStruct((t, d), BF16)],
        compiler_params=_cparams("parallel"),
        name="oproj",
    )(mix, w_o, x, g_next.reshape(1, d))


def _ffn_kernel(h_ref, wg_ref, wu_ref, wd_ref, x_ref, g_ref, x2_ref, h3_ref, acc_ref):
    f = pl.program_id(1)

    @pl.when(f == 0)
    def _():
        acc_ref[...] = x_ref[...]

    h = h_ref[...]
    g = jnp.dot(h, wg_ref[...], preferred_element_type=F32)
    u = jnp.dot(h, wu_ref[...], preferred_element_type=F32)
    act = (jax.nn.silu(g) * u).astype(BF16)
    acc_ref[...] += jnp.dot(act, wd_ref[...], preferred_element_type=F32)

    @pl.when(f == pl.num_programs(1) - 1)
    def _():
        x2 = acc_ref[...]
        x2_ref[...] = x2
        ms = jnp.mean(x2 * x2, axis=-1, keepdims=True)
        h3_ref[...] = (x2 * lax.rsqrt(ms + EPS) * g_ref[...]).astype(h3_ref.dtype)


def _ffn(h2, w_gu, w_d, x1, g_next, tm=512, tf=512):
    t, d = x1.shape
    d_ff = w_d.shape[0]
    tm = _tile(t, tm)
    tf = _tile(d_ff, tf)
    nf = d_ff // tf
    return pl.pallas_call(
        _ffn_kernel,
        grid=(t // tm, d_ff // tf),
        in_specs=[pl.BlockSpec((tm, d), lambda i, f: (i, 0)),
                  pl.BlockSpec((d, tf), lambda i, f: (0, f)),
                  pl.BlockSpec((d, tf), lambda i, f: (0, nf + f)),
                  pl.BlockSpec((tf, d), lambda i, f: (f, 0)),
                  pl.BlockSpec((tm, d), lambda i, f: (i, 0)),
                  pl.BlockSpec((1, d), lambda i, f: (0, 0))],
        out_specs=[pl.BlockSpec((tm, d), lambda i, f: (i, 0)),
                   pl.BlockSpec((tm, d), lambda i, f: (i, 0))],
        out_shape=[jax.ShapeDtypeStruct((t, d), F32), jax.ShapeDtypeStruct((t, d), BF16)],
        scratch_shapes=[pltpu.VMEM((tm, d), F32)],
        compiler_params=_cparams("parallel", "arbitrary"),
        name="ffn",
    )(h2, w_gu, w_gu, w_d, x1, g_next.reshape(1, d))


def _ple_kernel(h_ref, wg_ref, p_ref, wp_ref, x_ref, o_ref):
    gate = jax.nn.sigmoid(jnp.dot(h_ref[...], wg_ref[...], preferred_element_type=F32))
    emb = jnp.dot(p_ref[...], wp_ref[...], preferred_element_type=F32)
    o_ref[...] = x_ref[...] + gate * emb


def _ple(h3, w_gate, p, w_proj, x2, tm=1024, tn=1024):
    t, d = x2.shape
    dp = p.shape[1]
    tm = _tile(t, tm)
    tn = _tile(d, tn)
    return pl.pallas_call(
        _ple_kernel,
        grid=(t // tm, d // tn),
        in_specs=[pl.BlockSpec((tm, d), lambda i, j: (i, 0)),
                  pl.BlockSpec((d, tn), lambda i, j: (0, j)),
                  pl.BlockSpec((tm, dp), lambda i, j: (i, 0)),
                  pl.BlockSpec((dp, tn), lambda i, j: (0, j)),
                  pl.BlockSpec((tm, tn), lambda i, j: (i, j))],
        out_specs=pl.BlockSpec((tm, tn), lambda i, j: (i, j)),
        out_shape=jax.ShapeDtypeStruct((t, d), F32),
        compiler_params=_cparams("parallel", "parallel"),
        name="ple",
    )(h3, w_gate, p, w_proj, x2)


def _layer(x, p, g_mix, w_in, conv_w, conv_b, w_rg_a, b_rg_a, w_rg_x, b_rg_x, lru_lambda, q_gain,
           k_gain, w_rnn_out, w_attn_out, w_o, g_ffn, w_ffn_gu, w_ffn_down, g_ple, w_ple_gate,
           w_ple_proj):
    b, s, d = x.shape
    t = b * s
    d_rnn = w_rnn_out.shape[0]
    d_attn = w_attn_out.shape[0]
    x2d = x.reshape(t, d)

    o0 = 0
    o1 = o0 + d_rnn
    o2 = o1 + d_rnn
    o3 = o2 + 2 * d_attn
    o4 = o3 + d_attn
    o5 = o4 + 2 * d
    assert o5 == w_in.shape[1]

    h = _rmsnorm(x2d, g_mix)
    xr, = _proj(h, w_in, o0, d_rnn, "none", F32, tn=d_rnn // 2)
    gg, = _proj(h, w_in, o1, d_rnn, "gelu", F32, tn=d_rnn // 2)
    qk_gain = jnp.concatenate([jnp.tile(q_gain * (1.0 / math.sqrt(HEAD_DIM)), N_HEADS),
                               jnp.tile(k_gain, N_HEADS)]).reshape(1, 2 * d_attn)
    qk, w_gu_b, w_attn_out_b = _proj(h, w_in, o2, 2 * d_attn, "headnorm", BF16, gain=qk_gain,
                                     casts=(w_ffn_gu, w_attn_out))
    v, w_rnn_out_b = _proj(h, w_in, o3, d_attn, "none", BF16, casts=(w_rnn_out,))
    gates, w_down_b, w_o_b, w_ple_gate_b = _proj(h, w_in, o4, 2 * d, "sigmoid", BF16,
                                                 casts=(w_ffn_down, w_o, w_ple_gate))

    wband = _band_gate_weights(w_rg_a, w_rg_x)
    yr = _rnn(xr.reshape(b, s, d_rnn), gg.reshape(b, s, d_rnn), conv_w, conv_b, wband,
              b_rg_a, b_rg_x, lru_lambda)
    oa = _attention(qk.reshape(b, s, 2 * d_attn), v.reshape(b, s, d_attn))

    mix = _mix(yr.reshape(t, d_rnn), oa.reshape(t, d_attn), w_rnn_out_b, w_attn_out_b, gates)
    x1, h2 = _oproj(mix, w_o_b, x2d, g_ffn)
    x2, h3 = _ffn(h2, w_gu_b, w_down_b, x1, g_ple)
    out = _ple(h3, w_ple_gate_b, p.reshape(t, -1).astype(BF16), w_ple_proj.astype(BF16), x2)
    return out.reshape(b, s, d)


def kernel(x, p, g_mix, w_in, conv_w, conv_b, w_rg_a, b_rg_a, w_rg_x, b_rg_x, lru_lambda, q_gain,
           k_gain, w_rnn_out, w_attn_out, w_o, g_ffn, w_ffn_gu, w_ffn_down, g_ple, w_ple_gate,
           w_ple_proj):
    params = (g_mix, w_in, conv_w, conv_b, w_rg_a, b_rg_a, w_rg_x, b_rg_x, lru_lambda, q_gain,
              k_gain, w_rnn_out, w_attn_out, w_o, g_ffn, w_ffn_gu, w_ffn_down, g_ple, w_ple_gate,
              w_ple_proj)
    for i in range(p.shape[0]):
        x = _layer(x, p[i], *[a[i] for a in params])
    return x
```

```python
import functools
import math

import jax
import jax.numpy as jnp
import numpy as np
from jax import lax
from jax.experimental import pallas as pl
from jax.experimental.pallas import tpu as pltpu

F32 = jnp.float32
BF16 = jnp.bfloat16

EPS = 1e-6
LRU_C = 8.0
N_HEADS = 16
HEAD_DIM = 128
N_RNN_BLOCKS = 16
CONV_WIDTH = 4

LANES = 128
SUBLANES = 8
GATE_WINDOW = 4 * LANES
VMEM_LIMIT = 56 * 1024 * 1024
F32_TINY = float(np.finfo(np.float32).tiny)


def _cparams(*sem):
    return pltpu.CompilerParams(dimension_semantics=sem, vmem_limit_bytes=VMEM_LIMIT)


def _tile(n, pref):
    t = min(n, pref)
    assert n % t == 0, (n, t)
    return t


def _rmsnorm_kernel(x_ref, g_ref, o_ref):
    x = x_ref[...]
    ms = jnp.mean(x * x, axis=-1, keepdims=True)
    o_ref[...] = (x * lax.rsqrt(ms + EPS) * g_ref[...]).astype(o_ref.dtype)


def _rmsnorm(x, g, tm=512):
    t, d = x.shape
    tm = _tile(t, tm)
    return pl.pallas_call(
        _rmsnorm_kernel,
        grid=(t // tm,),
        in_specs=[pl.BlockSpec((tm, d), lambda i: (i, 0)),
                  pl.BlockSpec((1, d), lambda i: (0, 0))],
        out_specs=pl.BlockSpec((tm, d), lambda i: (i, 0)),
        out_shape=jax.ShapeDtypeStruct((t, d), BF16),
        compiler_params=_cparams("parallel"),
        name="rmsnorm",
    )(x, g.reshape(1, d))


def _head_norm(acc, gain):
    outs = []
    for c in range(acc.shape[1] // HEAD_DIM):
        blk = acc[:, c * HEAD_DIM:(c + 1) * HEAD_DIM]
        ms = jnp.mean(blk * blk, axis=-1, keepdims=True)
        outs.append(blk * lax.rsqrt(ms + EPS) * gain[:, c * HEAD_DIM:(c + 1) * HEAD_DIM])
    return jnp.concatenate(outs, axis=1)


PROJ_ROW_SLABS = 4


def _proj_kernel(h_ref, w_ref, *rest, kind, n_cast):
    wb_ref = rest[-1]
    cast_out = rest[len(rest) - 1 - n_cast:-1]
    o_ref = rest[-2 - n_cast]
    cast_in = rest[-2 - 2 * n_cast:-2 - n_cast]
    extra = rest[:-2 - 2 * n_cast]

    @pl.when(pl.program_id(1) == 0)
    def _():
        wb_ref[...] = w_ref[...].astype(BF16)

    rows = h_ref.shape[0] // PROJ_ROW_SLABS
    for m in range(PROJ_ROW_SLABS):
        sl = slice(m * rows, (m + 1) * rows)
        acc = jnp.dot(h_ref[sl, :], wb_ref[...], preferred_element_type=F32)
        if kind == "gelu":
            acc = jax.nn.gelu(acc)
        elif kind == "sigmoid":
            acc = jax.nn.sigmoid(acc)
        elif kind == "headnorm":
            acc = _head_norm(acc, extra[0][...])
        o_ref[sl, :] = acc.astype(o_ref.dtype)

    for src_ref, dst_ref in zip(cast_in, cast_out):
        dst_ref[...] = src_ref[...].astype(BF16)


def _proj(h, w, col0, n, kind, out_dtype, gain=None, casts=(), tm=1024, tn=1024):
    t, d = h.shape
    tm = _tile(t, tm)
    tn = _tile(n, tn)
    assert col0 % LANES == 0 and tn % LANES == 0 and tm % (PROJ_ROW_SLABS * 2 * SUBLANES) == 0
    n_i = t // tm
    n_steps = (n // tn) * n_i
    in_specs = [pl.BlockSpec((tm, d), lambda j, i: (i, 0)),
                pl.BlockSpec((pl.Element(d), pl.Element(tn)),
                             lambda j, i: (0, pl.multiple_of(col0 + j * tn, LANES)))]
    args = [h, w]
    if gain is not None:
        in_specs.append(pl.BlockSpec((1, tn), lambda j, i: (0, j)))
        args.append(gain)
    out_specs = [pl.BlockSpec((tm, tn), lambda j, i: (i, j))]
    out_shape = [jax.ShapeDtypeStruct((t, n), out_dtype)]
    for c in casts:
        rows = c.shape[0] // n_steps
        assert rows * n_steps == c.shape[0] and rows % (2 * SUBLANES) == 0, (c.shape, n_steps)
        spec = pl.BlockSpec((rows, c.shape[1]), lambda j, i: (j * n_i + i, 0))
        in_specs.append(spec)
        args.append(c)
        out_specs.append(spec)
        out_shape.append(jax.ShapeDtypeStruct(c.shape, BF16))
    return pl.pallas_call(
        functools.partial(_proj_kernel, kind=kind, n_cast=len(casts)),
        grid=(n // tn, n_i),
        in_specs=in_specs,
        out_specs=out_specs,
        out_shape=out_shape,
        scratch_shapes=[pltpu.VMEM((d, tn), BF16)],
        compiler_params=_cparams("parallel", "arbitrary"),
        name="proj_" + kind,
    )(*args)


def _gate_window_starts(d_rnn):
    blk = d_rnn // N_RNN_BLOCKS
    n_tiles = d_rnn // LANES
    starts = []
    for j in range(n_tiles):
        b0 = (LANES * j) // blk
        b1 = (LANES * j + LANES - 1) // blk
        s = min((blk * b0) // LANES, n_tiles - GATE_WINDOW // LANES)
        assert s * LANES <= blk * b0 and blk * (b1 + 1) <= s * LANES + GATE_WINDOW
        starts.append(s)
    return starts


def _band_gate_weights(w_a, w_x):
    nb, blk, _ = w_a.shape
    d_rnn = nb * blk
    starts = _gate_window_starts(d_rnn)

    def band(w):
        tiles = []
        for j, s in enumerate(starts):
            c_lo, c_hi = LANES * j, LANES * (j + 1)
            pieces = []
            for b in range(c_lo // blk, (c_hi - 1) // blk + 1):
                oc0, oc1 = max(c_lo, blk * b), min(c_hi, blk * (b + 1))
                r0 = blk * b - s * LANES
                pieces.append(jnp.pad(w[b, :, oc0 - blk * b:oc1 - blk * b],
                                      ((r0, GATE_WINDOW - blk - r0), (0, 0))))
            tiles.append(jnp.concatenate(pieces, axis=1))
        return jnp.stack(tiles)

    return jnp.concatenate([band(w_a), band(w_x)], axis=-1).astype(BF16)


def _rnn_kernel(xr_ref, gg_ref, cw_ref, cb_ref, wband_ref, ba_ref, bx_ref, lam_ref, y_ref,
                tail_ref, xin_ref, hout_ref, xc_ref, xcb_ref, hcar_ref, *, tc, d_rnn, starts):
    n_tiles = d_rnn // LANES
    seg = tc // SUBLANES
    pitch = seg + SUBLANES

    @pl.when(pl.program_id(1) == 0)
    def _():
        tail_ref[...] = jnp.zeros_like(tail_ref)
        hcar_ref[...] = jnp.zeros_like(hcar_ref)

    sub = lax.broadcasted_iota(jnp.int32, (SUBLANES, LANES), 0)

    def seg_rows(p):
        return slice(p * seg, (p + 1) * seg)

    def slab_rows(p):
        return slice(p * pitch, p * pitch + seg)

    for j in range(n_tiles):
        cols = slice(j * LANES, (j + 1) * LANES)
        for p in range(SUBLANES):
            xin_ref[j, slab_rows(p), :] = xr_ref[seg_rows(p), cols]
        x = [xin_ref[j, pl.ds(t, SUBLANES, stride=pitch), :] for t in range(seg)]
        half_w = [0.5 * cw_ref[k:k + 1, cols] for k in range(CONV_WIDTH)]
        half_b = 0.5 * cb_ref[:, cols]

        def shifted(t, s):
            if t >= s:
                return x[t - s]
            i = SUBLANES - s + t
            prev = jnp.broadcast_to(tail_ref[i:i + 1, cols], (SUBLANES, LANES))
            return jnp.where(sub >= 1, pltpu.roll(x[seg + t - s], 1, axis=0), prev)

        xc = []
        for t in range(seg):
            acc = half_b + half_w[CONV_WIDTH - 1] * x[t]
            for s in range(1, CONV_WIDTH):
                acc = acc + half_w[CONV_WIDTH - 1 - s] * shifted(t, s)
            xc.append(acc)
        xc = jnp.concatenate(xc, axis=0)
        xc_ref[:, cols] = xc
        xcb_ref[:, cols] = xc.astype(BF16)

    tail_ref[...] = xr_ref[tc - SUBLANES:tc, :]

    for j in range(n_tiles):
        cols = slice(j * LANES, (j + 1) * LANES)
        ks = starts[j] * LANES
        pre = jnp.dot(xcb_ref[:, ks:ks + GATE_WINDOW], wband_ref[j], preferred_element_type=F32)
        tanh_r = jnp.tanh(pre[:, :LANES] + 0.5 * ba_ref[:, cols])
        tanh_i = jnp.tanh(pre[:, LANES:] + 0.5 * bx_ref[:, cols])
        nlam = -lam_ref[:, cols]
        softplus = jnp.maximum(nlam, 0.0) + jnp.log1p(jnp.exp(-jnp.abs(nlam)))
        half_c = (-0.5 * LRU_C) * softplus
        log_a = half_c * tanh_r + half_c
        a = jnp.exp(log_a)
        y = -jnp.tanh(log_a) * (a * a + 1.0)
        gated_x = (tanh_i + 1.0) * xc_ref[:, cols]
        u = y * lax.rsqrt(jnp.maximum(y, F32_TINY)) * gated_x
        a = a.reshape(seg, SUBLANES, LANES)
        u = u.reshape(seg, SUBLANES, LANES)

        h = u[0]
        decay = a[0]
        for t in range(1, seg):
            h = a[t] * h + u[t]
            decay = decay * a[t]
        for d in (1, 2, 4):
            m = sub >= d
            h = h + decay * jnp.where(m, pltpu.roll(h, d, axis=0), 0.0)
            decay = decay * jnp.where(m, pltpu.roll(decay, d, axis=0), 1.0)
        h_in = hcar_ref[:, cols]
        h_end = h + decay * h_in
        hcar_ref[:, cols] = jnp.broadcast_to(h_end[SUBLANES - 1:SUBLANES, :], (SUBLANES, LANES))
        h = jnp.where(sub >= 1, pltpu.roll(h_end, 1, axis=0), h_in)
        for t in range(seg):
            h = a[t] * h + u[t]
            hout_ref[j, pl.ds(t, SUBLANES, stride=pitch), :] = h
        for p in range(SUBLANES):
            y_ref[seg_rows(p), cols] = (gg_ref[seg_rows(p), cols] * hout_ref[j, slab_rows(p), :]).astype(y_ref.dtype)


def _rnn(xr, gg, conv_w, conv_b, wband, b_a, b_x, lam, tc=256):
    b, s, d_rnn = xr.shape
    tc = _tile(s, tc)
    starts = _gate_window_starts(d_rnn)
    n_tiles = d_rnn // LANES
    slab = (n_tiles, tc + SUBLANES * SUBLANES, LANES)
    row = lambda v: v.reshape(1, d_rnn)
    full2 = lambda shape: pl.BlockSpec(shape, lambda bi, ci: (0, 0))
    chunk = pl.BlockSpec((None, tc, d_rnn), lambda bi, ci: (bi, ci, 0))
    return pl.pallas_call(
        functools.partial(_rnn_kernel, tc=tc, d_rnn=d_rnn, starts=starts),
        grid=(b, s // tc),
        in_specs=[chunk, chunk,
                  full2((CONV_WIDTH, d_rnn)), full2((1, d_rnn)),
                  pl.BlockSpec(wband.shape, lambda bi, ci: (0, 0, 0)),
                  full2((1, d_rnn)), full2((1, d_rnn)), full2((1, d_rnn))],
        out_specs=chunk,
        out_shape=jax.ShapeDtypeStruct((b, s, d_rnn), BF16),
        scratch_shapes=[pltpu.VMEM((SUBLANES, d_rnn), F32),
                        pltpu.VMEM(slab, F32),
                        pltpu.VMEM(slab, F32),
                        pltpu.VMEM((tc, d_rnn), F32),
                        pltpu.VMEM((tc, d_rnn), BF16),
                        pltpu.VMEM((SUBLANES, d_rnn), F32)],
        compiler_params=_cparams("parallel", "arbitrary"),
        name="rnn",
    )(xr, gg, conv_w, row(conv_b), wband, row(b_a), row(b_x), row(lam))


PRUNE_LOG = -104.0
SOFTPLUS_LINEAR = 80.0


def _attn_kernel(q_ref, k_ref, v_ref, tri_ref, o_ref, *, tq, hp):
    qi = pl.program_id(2)
    row = lax.broadcasted_iota(jnp.int32, (tq, tq), 0)
    col = lax.broadcasted_iota(jnp.int32, (tq, tq), 1)
    causal = col < row

    def sweep(last_kb, n_blocks, diag, state):
        rows = pl.ds(pl.multiple_of((last_kb - (n_blocks - 1)) * tq, tq), n_blocks * tq)
        zs, sps = [], []
        for hd in range(hp):
            lanes = slice(hd * HEAD_DIM, (hd + 1) * HEAD_DIM)
            z = lax.dot_general(q_ref[:, lanes], k_ref[rows, lanes], (((1,), (1,)), ((), ())),
                                preferred_element_type=F32)
            for blk in range(n_blocks):
                zb = z[:, blk * tq:(blk + 1) * tq]
                sp = jnp.where(zb > SOFTPLUS_LINEAR, zb, jnp.log(1.0 + jnp.exp(zb)))
                if diag and blk == n_blocks - 1:
                    sp = jnp.where(causal, sp, 0.0)
                zs.append(zb)
                sps.append(sp)
        excl = jnp.dot(jnp.concatenate([sp.astype(BF16) for sp in sps], axis=0), tri_ref[...],
                       preferred_element_type=F32)
        out = []
        for hd in range(hp):
            lanes = slice(hd * HEAD_DIM, (hd + 1) * HEAD_DIM)
            run, acc = state[hd]
            ws = [None] * n_blocks
            for blk in reversed(range(n_blocks)):
                u = hd * n_blocks + blk
                w = jnp.exp((zs[u] - sps[u]) + excl[u * tq:(u + 1) * tq] + run)
                if diag and blk == n_blocks - 1:
                    w = jnp.where(causal, w, 0.0)
                ws[blk] = w.astype(BF16)
                run = run - jnp.sum(sps[u], axis=-1, keepdims=True)
            acc = acc + jnp.dot(jnp.concatenate(ws, axis=1), v_ref[rows, lanes], preferred_element_type=F32)
            out.append((run, acc))
        return tuple(out)

    def run_max(state):
        m = jnp.max(state[0][0])
        for run, _ in state[1:]:
            m = jnp.maximum(m, jnp.max(run))
        return m

    zero = tuple((jnp.zeros((tq, 1), F32), jnp.zeros((tq, HEAD_DIM), F32)) for _ in range(hp))

    def store(state):
        for hd, (_, acc) in enumerate(state):
            o_ref[:, hd * HEAD_DIM:(hd + 1) * HEAD_DIM] = acc.astype(o_ref.dtype)

    @pl.when(qi == 0)
    def _():
        store(sweep(qi, 1, True, zero))

    @pl.when(qi > 0)
    def _():
        state = sweep(qi, 2, True, zero)

        def cond(c):
            kb, rmax, _ = c
            return jnp.logical_and(kb >= 0, rmax > PRUNE_LOG)

        def body(c):
            kb, _, st = c
            st = sweep(kb, 1, False, st)
            return kb - 1, run_max(st), st

        _, _, state = lax.while_loop(cond, body, (qi - 2, run_max(state), state))
        store(state)


def _attention(qk, v, tq=256, hp=8):
    b, s, _ = v.shape
    tq = _tile(s, tq)
    hw = hp * HEAD_DIM
    n_hg = N_HEADS // hp
    tri = jnp.asarray(-np.tril(np.ones((tq, tq), np.float32), -1), BF16)
    return pl.pallas_call(
        functools.partial(_attn_kernel, tq=tq, hp=hp),
        grid=(b, n_hg, s // tq),
        in_specs=[pl.BlockSpec((None, tq, hw), lambda bi, h, qi: (bi, qi, h)),
                  pl.BlockSpec((None, s, hw), lambda bi, h, qi: (bi, 0, n_hg + h)),
                  pl.BlockSpec((None, s, hw), lambda bi, h, qi: (bi, 0, h)),
                  pl.BlockSpec((tq, tq), lambda bi, h, qi: (0, 0))],
        out_specs=pl.BlockSpec((None, tq, hw), lambda bi, h, qi: (bi, qi, h)),
        out_shape=jax.ShapeDtypeStruct(v.shape, BF16),
        compiler_params=_cparams("parallel", "parallel", "arbitrary"),
        name="attention",
    )(qk, qk, v, tri)


def _mix_kernel(yr_ref, oa_ref, wr_ref, wa_ref, sr_ref, sa_ref, o_ref):
    ya = jnp.dot(yr_ref[...], wr_ref[...], preferred_element_type=F32)
    yb = jnp.dot(oa_ref[...], wa_ref[...], preferred_element_type=F32)
    o_ref[...] = (sr_ref[...] * ya + sa_ref[...] * yb).astype(o_ref.dtype)


def _mix(yr, oa, w_r, w_a, gates, tm=1024, tn=1024):
    t, d_rnn = yr.shape
    d_attn = oa.shape[1]
    n = w_r.shape[1]
    tm = _tile(t, tm)
    tn = _tile(n, tn)
    nj = n // tn
    return pl.pallas_call(
        _mix_kernel,
        grid=(nj, t // tm),
        in_specs=[pl.BlockSpec((tm, d_rnn), lambda j, i: (i, 0)),
                  pl.BlockSpec((tm, d_attn), lambda j, i: (i, 0)),
                  pl.BlockSpec((d_rnn, tn), lambda j, i: (0, j), pipeline_mode=pl.Buffered(1)),
                  pl.BlockSpec((d_attn, tn), lambda j, i: (0, j), pipeline_mode=pl.Buffered(1)),
                  pl.BlockSpec((tm, tn), lambda j, i: (i, j)),
                  pl.BlockSpec((tm, tn), lambda j, i: (i, nj + j))],
        out_specs=pl.BlockSpec((tm, tn), lambda j, i: (i, j)),
        out_shape=jax.ShapeDtypeStruct((t, n), BF16),
        compiler_params=_cparams("parallel", "parallel"),
        name="mix",
    )(yr, oa, w_r, w_a, gates, gates)


def _oproj_kernel(m_ref, w_ref, x_ref, g_ref, x1_ref, h_ref):
    x1 = x_ref[...] + jnp.dot(m_ref[...], w_ref[...], preferred_element_type=F32)
    x1_ref[...] = x1
    ms = jnp.mean(x1 * x1, axis=-1, keepdims=True)
    h_ref[...] = (x1 * lax.rsqrt(ms + EPS) * g_ref[...]).astype(h_ref.dtype)


def _oproj(mix, w_o, x, g_next, tm=512):
    t, d = x.shape
    tm = _tile(t, tm)
    return pl.pallas_call(
        _oproj_kernel,
        grid=(t // tm,),
        in_specs=[pl.BlockSpec((tm, d), lambda i: (i, 0)),
                  pl.BlockSpec((d, d), lambda i: (0, 0)),
                  pl.BlockSpec((tm, d), lambda i: (i, 0)),
                  pl.BlockSpec((1, d), lambda i: (0, 0))],
        out_specs=[pl.BlockSpec((tm, d), lambda i: (i, 0)),
                   pl.BlockSpec((tm, d), lambda i: (i, 0))],
        out_shape=[jax.ShapeDtypeStruct((t, d), F32), jax.ShapeDtypeStruct((t, d), BF16)],
        compiler_params=_cparams("parallel"),
        name="oproj",
    )(mix, w_o, x, g_next.reshape(1, d))


def _ffn_kernel(h_ref, wg_ref, wu_ref, wd_ref, x_ref, g_ref, x2_ref, h3_ref, acc_ref):
    f = pl.program_id(1)

    @pl.when(f == 0)
    def _():
        acc_ref[...] = x_ref[...]

    h = h_ref[...]
    g = jnp.dot(h, wg_ref[...], preferred_element_type=F32)
    u = jnp.dot(h, wu_ref[...], preferred_element_type=F32)
    act = (jax.nn.silu(g) * u).astype(BF16)
    acc_ref[...] += jnp.dot(act, wd_ref[...], preferred_element_type=F32)

    @pl.when(f == pl.num_programs(1) - 1)
    def _():
        x2 = acc_ref[...]
        x2_ref[...] = x2
        ms = jnp.mean(x2 * x2, axis=-1, keepdims=True)
        h3_ref[...] = (x2 * lax.rsqrt(ms + EPS) * g_ref[...]).astype(h3_ref.dtype)


def _ffn(h2, w_gu, w_d, x1, g_next, tm=512, tf=512):
    t, d = x1.shape
    d_ff = w_d.shape[0]
    tm = _tile(t, tm)
    tf = _tile(d_ff, tf)
    nf = d_ff // tf
    return pl.pallas_call(
        _ffn_kernel,
        grid=(t // tm, d_ff // tf),
        in_specs=[pl.BlockSpec((tm, d), lambda i, f: (i, 0)),
                  pl.BlockSpec((d, tf), lambda i, f: (0, f)),
                  pl.BlockSpec((d, tf), lambda i, f: (0, nf + f)),
                  pl.BlockSpec((tf, d), lambda i, f: (f, 0)),
                  pl.BlockSpec((tm, d), lambda i, f: (i, 0)),
                  pl.BlockSpec((1, d), lambda i, f: (0, 0))],
        out_specs=[pl.BlockSpec((tm, d), lambda i, f: (i, 0)),
                   pl.BlockSpec((tm, d), lambda i, f: (i, 0))],
        out_shape=[jax.ShapeDtypeStruct((t, d), F32), jax.ShapeDtypeStruct((t, d), BF16)],
        scratch_shapes=[pltpu.VMEM((tm, d), F32)],
        compiler_params=_cparams("parallel", "arbitrary"),
        name="ffn",
    )(h2, w_gu, w_gu, w_d, x1, g_next.reshape(1, d))


def _ple_kernel(h_ref, wg_ref, p_ref, wp_ref, x_ref, o_ref):
    rows = h_ref.shape[0] // PROJ_ROW_SLABS
    for m in range(PROJ_ROW_SLABS):
        sl = slice(m * rows, (m + 1) * rows)
        gate = jax.nn.sigmoid(jnp.dot(h_ref[sl, :], wg_ref[...], preferred_element_type=F32))
        emb = jnp.dot(p_ref[sl, :], wp_ref[...], preferred_element_type=F32)
        o_ref[sl, :] = x_ref[sl, :] + gate * emb


def _ple(h3, w_gate, p, w_proj, x2, tm=1024, tn=1024):
    t, d = x2.shape
    dp = p.shape[1]
    tm = _tile(t, tm)
    tn = _tile(d, tn)
    return pl.pallas_call(
        _ple_kernel,
        grid=(t // tm, d // tn),
        in_specs=[pl.BlockSpec((tm, d), lambda i, j: (i, 0)),
                  pl.BlockSpec((d, tn), lambda i, j: (0, j)),
                  pl.BlockSpec((tm, dp), lambda i, j: (i, 0)),
                  pl.BlockSpec((dp, tn), lambda i, j: (0, j)),
                  pl.BlockSpec((tm, tn), lambda i, j: (i, j))],
        out_specs=pl.BlockSpec((tm, tn), lambda i, j: (i, j)),
        out_shape=jax.ShapeDtypeStruct((t, d), F32),
        compiler_params=_cparams("parallel", "parallel"),
        name="ple",
    )(h3, w_gate, p, w_proj, x2)


def _layer(x, p, g_mix, w_in, conv_w, conv_b, w_rg_a, b_rg_a, w_rg_x, b_rg_x, lru_lambda, q_gain,
           k_gain, w_rnn_out, w_attn_out, w_o, g_ffn, w_ffn_gu, w_ffn_down, g_ple, w_ple_gate,
           w_ple_proj):
    b, s, d = x.shape
    t = b * s
    d_rnn = w_rnn_out.shape[0]
    d_attn = w_attn_out.shape[0]
    x2d = x.reshape(t, d)

    o0 = 0
    o1 = o0 + d_rnn
    o2 = o1 + d_rnn
    o3 = o2 + 2 * d_attn
    o4 = o3 + d_attn
    o5 = o4 + 2 * d
    assert o5 == w_in.shape[1]

    h = _rmsnorm(x2d, g_mix)
    xr, = _proj(h, w_in, o0, d_rnn, "none", F32, tn=d_rnn // 2)
    gg, = _proj(h, w_in, o1, d_rnn, "gelu", F32, tn=d_rnn // 2)
    qk_gain = jnp.concatenate([jnp.tile(q_gain * (1.0 / math.sqrt(HEAD_DIM)), N_HEADS),
                               jnp.tile(k_gain, N_HEADS)]).reshape(1, 2 * d_attn)
    qk, w_gu_b, w_attn_out_b = _proj(h, w_in, o2, 2 * d_attn, "headnorm", BF16, gain=qk_gain,
                                     casts=(w_ffn_gu, w_attn_out))
    v, w_rnn_out_b = _proj(h, w_in, o3, d_attn, "none", BF16, casts=(w_rnn_out,))
    gates, w_down_b, w_o_b, w_ple_gate_b = _proj(h, w_in, o4, 2 * d, "sigmoid", BF16,
                                                 casts=(w_ffn_down, w_o, w_ple_gate))

    wband = _band_gate_weights(w_rg_a, w_rg_x)
    yr = _rnn(xr.reshape(b, s, d_rnn), gg.reshape(b, s, d_rnn), conv_w, conv_b, wband,
              b_rg_a, b_rg_x, lru_lambda)
    oa = _attention(qk.reshape(b, s, 2 * d_attn), v.reshape(b, s, d_attn))

    mix = _mix(yr.reshape(t, d_rnn), oa.reshape(t, d_attn), w_rnn_out_b, w_attn_out_b, gates)
    x1, h2 = _oproj(mix, w_o_b, x2d, g_ffn)
    x2, h3 = _ffn(h2, w_gu_b, w_down_b, x1, g_ple)
    out = _ple(h3, w_ple_gate_b, p.reshape(t, -1).astype(BF16), w_ple_proj.astype(BF16), x2)
    return out.reshape(b, s, d)


def kernel(x, p, g_mix, w_in, conv_w, conv_b, w_rg_a, b_rg_a, w_rg_x, b_rg_x, lru_lambda, q_gain,
           k_gain, w_rnn_out, w_attn_out, w_o, g_ffn, w_ffn_gu, w_ffn_down, g_ple, w_ple_gate,
           w_ple_proj):
    params = (g_mix, w_in, conv_w, conv_b, w_rg_a, b_rg_a, w_rg_x, b_rg_x, lru_lambda, q_gain,
              k_gain, w_rnn_out, w_attn_out, w_o, g_ffn, w_ffn_gu, w_ffn_down, g_ple, w_ple_gate,
              w_ple_proj)
    for i in range(p.shape[0]):
        x = _layer(x, p[i], *[a[i] for a in params])
    return x
```

```python
import functools
import math

import jax
import jax.numpy as jnp
import numpy as np
from jax import lax
from jax.experimental import pallas as pl
from jax.experimental.pallas import tpu as pltpu

F32 = jnp.float32
BF16 = jnp.bfloat16

EPS = 1e-6
LRU_C = 8.0
N_HEADS = 16
HEAD_DIM = 128
N_RNN_BLOCKS = 16
CONV_WIDTH = 4

LANES = 128
SUBLANES = 8
GATE_WINDOW = 4 * LANES
VMEM_LIMIT = 56 * 1024 * 1024
F32_TINY = float(np.finfo(np.float32).tiny)


def _cparams(*sem):
    return pltpu.CompilerParams(dimension_semantics=sem, vmem_limit_bytes=VMEM_LIMIT)


def _tile(n, pref):
    t = min(n, pref)
    assert n % t == 0, (n, t)
    return t


def _rmsnorm_kernel(x_ref, g_ref, o_ref):
    x = x_ref[...]
    ms = jnp.mean(x * x, axis=-1, keepdims=True)
    o_ref[...] = (x * lax.rsqrt(ms + EPS) * g_ref[...]).astype(o_ref.dtype)


def _rmsnorm(x, g, tm=512):
    t, d = x.shape
    tm = _tile(t, tm)
    return pl.pallas_call(
        _rmsnorm_kernel,
        grid=(t // tm,),
        in_specs=[pl.BlockSpec((tm, d), lambda i: (i, 0)),
                  pl.BlockSpec((1, d), lambda i: (0, 0))],
        out_specs=pl.BlockSpec((tm, d), lambda i: (i, 0)),
        out_shape=jax.ShapeDtypeStruct((t, d), BF16),
        compiler_params=_cparams("parallel"),
        name="rmsnorm",
    )(x, g.reshape(1, d))


def _head_norm(acc, gain):
    outs = []
    for c in range(acc.shape[1] // HEAD_DIM):
        blk = acc[:, c * HEAD_DIM:(c + 1) * HEAD_DIM]
        ms = jnp.mean(blk * blk, axis=-1, keepdims=True)
        outs.append(blk * lax.rsqrt(ms + EPS) * gain[:, c * HEAD_DIM:(c + 1) * HEAD_DIM])
    return jnp.concatenate(outs, axis=1)


PROJ_ROW_SLABS = 4


def _proj_kernel(h_ref, w_ref, *rest, kind, n_cast):
    wb_ref = rest[-1]
    cast_out = rest[len(rest) - 1 - n_cast:-1]
    o_ref = rest[-2 - n_cast]
    cast_in = rest[-2 - 2 * n_cast:-2 - n_cast]
    extra = rest[:-2 - 2 * n_cast]

    @pl.when(pl.program_id(1) == 0)
    def _():
        wb_ref[...] = w_ref[...].astype(BF16)

    rows = h_ref.shape[0] // PROJ_ROW_SLABS
    for m in range(PROJ_ROW_SLABS):
        sl = slice(m * rows, (m + 1) * rows)
        acc = jnp.dot(h_ref[sl, :], wb_ref[...], preferred_element_type=F32)
        if kind == "gelu":
            acc = jax.nn.gelu(acc)
        elif kind == "sigmoid":
            acc = jax.nn.sigmoid(acc)
        elif kind == "headnorm":
            acc = _head_norm(acc, extra[0][...])
        o_ref[sl, :] = acc.astype(o_ref.dtype)

    for src_ref, dst_ref in zip(cast_in, cast_out):
        dst_ref[...] = src_ref[...].astype(BF16)


def _proj(h, w, col0, n, kind, out_dtype, gain=None, casts=(), tm=1024, tn=1024):
    t, d = h.shape
    tm = _tile(t, tm)
    tn = _tile(n, tn)
    assert col0 % LANES == 0 and tn % LANES == 0 and tm % (PROJ_ROW_SLABS * 2 * SUBLANES) == 0
    n_i = t // tm
    n_steps = (n // tn) * n_i
    in_specs = [pl.BlockSpec((tm, d), lambda j, i: (i, 0)),
                pl.BlockSpec((pl.Element(d), pl.Element(tn)),
                             lambda j, i: (0, pl.multiple_of(col0 + j * tn, LANES)))]
    args = [h, w]
    if gain is not None:
        in_specs.append(pl.BlockSpec((1, tn), lambda j, i: (0, j)))
        args.append(gain)
    out_specs = [pl.BlockSpec((tm, tn), lambda j, i: (i, j))]
    out_shape = [jax.ShapeDtypeStruct((t, n), out_dtype)]
    for c in casts:
        rows = c.shape[0] // n_steps
        assert rows * n_steps == c.shape[0] and rows % (2 * SUBLANES) == 0, (c.shape, n_steps)
        spec = pl.BlockSpec((rows, c.shape[1]), lambda j, i: (j * n_i + i, 0))
        in_specs.append(spec)
        args.append(c)
        out_specs.append(spec)
        out_shape.append(jax.ShapeDtypeStruct(c.shape, BF16))
    return pl.pallas_call(
        functools.partial(_proj_kernel, kind=kind, n_cast=len(casts)),
        grid=(n // tn, n_i),
        in_specs=in_specs,
        out_specs=out_specs,
        out_shape=out_shape,
        scratch_shapes=[pltpu.VMEM((d, tn), BF16)],
        compiler_params=_cparams("parallel", "arbitrary"),
        name="proj_" + kind,
    )(*args)


def _gate_window_starts(d_rnn):
    blk = d_rnn // N_RNN_BLOCKS
    n_tiles = d_rnn // LANES
    starts = []
    for j in range(n_tiles):
        b0 = (LANES * j) // blk
        b1 = (LANES * j + LANES - 1) // blk
        s = min((blk * b0) // LANES, n_tiles - GATE_WINDOW // LANES)
        assert s * LANES <= blk * b0 and blk * (b1 + 1) <= s * LANES + GATE_WINDOW
        starts.append(s)
    return starts


def _band_gate_weights(w_a, w_x):
    nb, blk, _ = w_a.shape
    d_rnn = nb * blk
    starts = _gate_window_starts(d_rnn)

    def band(w):
        tiles = []
        for j, s in enumerate(starts):
            c_lo, c_hi = LANES * j, LANES * (j + 1)
            pieces = []
            for b in range(c_lo // blk, (c_hi - 1) // blk + 1):
                oc0, oc1 = max(c_lo, blk * b), min(c_hi, blk * (b + 1))
                r0 = blk * b - s * LANES
                pieces.append(jnp.pad(w[b, :, oc0 - blk * b:oc1 - blk * b],
                                      ((r0, GATE_WINDOW - blk - r0), (0, 0))))
            tiles.append(jnp.concatenate(pieces, axis=1))
        return jnp.stack(tiles)

    return jnp.concatenate([band(w_a), band(w_x)], axis=-1).astype(BF16)


def _rnn_kernel(xr_ref, gg_ref, cw_ref, cb_ref, wband_ref, ba_ref, bx_ref, lam_ref, y_ref,
                tail_ref, xin_ref, hout_ref, xc_ref, xcb_ref, hcar_ref, *, tc, d_rnn, starts):
    n_tiles = d_rnn // LANES
    seg = tc // SUBLANES
    pitch = seg + SUBLANES

    @pl.when(pl.program_id(1) == 0)
    def _():
        tail_ref[...] = jnp.zeros_like(tail_ref)
        hcar_ref[...] = jnp.zeros_like(hcar_ref)

    sub = lax.broadcasted_iota(jnp.int32, (SUBLANES, LANES), 0)

    def seg_rows(p):
        return slice(p * seg, (p + 1) * seg)

    def slab_rows(p):
        return slice(p * pitch, p * pitch + seg)

    for j in range(n_tiles):
        cols = slice(j * LANES, (j + 1) * LANES)
        for p in range(SUBLANES):
            xin_ref[j, slab_rows(p), :] = xr_ref[seg_rows(p), cols]
        x = [xin_ref[j, pl.ds(t, SUBLANES, stride=pitch), :] for t in range(seg)]
        half_w = [0.5 * cw_ref[k:k + 1, cols] for k in range(CONV_WIDTH)]
        half_b = 0.5 * cb_ref[:, cols]

        def shifted(t, s):
            if t >= s:
                return x[t - s]
            i = SUBLANES - s + t
            prev = jnp.broadcast_to(tail_ref[i:i + 1, cols], (SUBLANES, LANES))
            return jnp.where(sub >= 1, pltpu.roll(x[seg + t - s], 1, axis=0), prev)

        xc = []
        for t in range(seg):
            acc = half_b + half_w[CONV_WIDTH - 1] * x[t]
            for s in range(1, CONV_WIDTH):
                acc = acc + half_w[CONV_WIDTH - 1 - s] * shifted(t, s)
            xc.append(acc)
        xc = jnp.concatenate(xc, axis=0)
        xc_ref[:, cols] = xc
        xcb_ref[:, cols] = xc.astype(BF16)

    tail_ref[...] = xr_ref[tc - SUBLANES:tc, :]

    for j in range(n_tiles):
        cols = slice(j * LANES, (j + 1) * LANES)
        ks = starts[j] * LANES
        pre = jnp.dot(xcb_ref[:, ks:ks + GATE_WINDOW], wband_ref[j], preferred_element_type=F32)
        tanh_r = jnp.tanh(pre[:, :LANES] + 0.5 * ba_ref[:, cols])
        tanh_i = jnp.tanh(pre[:, LANES:] + 0.5 * bx_ref[:, cols])
        nlam = -lam_ref[:, cols]
        softplus = jnp.maximum(nlam, 0.0) + jnp.log1p(jnp.exp(-jnp.abs(nlam)))
        half_c = (-0.5 * LRU_C) * softplus
        log_a = half_c * tanh_r + half_c
        a = jnp.exp(log_a)
        y = -jnp.tanh(log_a) * (a * a + 1.0)
        gated_x = (tanh_i + 1.0) * xc_ref[:, cols]
        u = y * lax.rsqrt(jnp.maximum(y, F32_TINY)) * gated_x
        a = a.reshape(seg, SUBLANES, LANES)
        u = u.reshape(seg, SUBLANES, LANES)

        h = u[0]
        decay = a[0]
        for t in range(1, seg):
            h = a[t] * h + u[t]
            decay = decay * a[t]
        for d in (1, 2, 4):
            m = sub >= d
            h = h + decay * jnp.where(m, pltpu.roll(h, d, axis=0), 0.0)
            decay = decay * jnp.where(m, pltpu.roll(decay, d, axis=0), 1.0)
        h_in = hcar_ref[:, cols]
        h_end = h + decay * h_in
        hcar_ref[:, cols] = jnp.broadcast_to(h_end[SUBLANES - 1:SUBLANES, :], (SUBLANES, LANES))
        h = jnp.where(sub >= 1, pltpu.roll(h_end, 1, axis=0), h_in)
        for t in range(seg):
            h = a[t] * h + u[t]
            hout_ref[j, pl.ds(t, SUBLANES, stride=pitch), :] = h
        for p in range(SUBLANES):
            y_ref[seg_rows(p), cols] = (gg_ref[seg_rows(p), cols] * hout_ref[j, slab_rows(p), :]).astype(y_ref.dtype)


def _rnn(xr, gg, conv_w, conv_b, wband, b_a, b_x, lam, tc=256):
    b, s, d_rnn = xr.shape
    tc = _tile(s, tc)
    starts = _gate_window_starts(d_rnn)
    n_tiles = d_rnn // LANES
    slab = (n_tiles, tc + SUBLANES * SUBLANES, LANES)
    row = lambda v: v.reshape(1, d_rnn)
    full2 = lambda shape: pl.BlockSpec(shape, lambda bi, ci: (0, 0))
    chunk = pl.BlockSpec((None, tc, d_rnn), lambda bi, ci: (bi, ci, 0))
    return pl.pallas_call(
        functools.partial(_rnn_kernel, tc=tc, d_rnn=d_rnn, starts=starts),
        grid=(b, s // tc),
        in_specs=[chunk, chunk,
                  full2((CONV_WIDTH, d_rnn)), full2((1, d_rnn)),
                  pl.BlockSpec(wband.shape, lambda bi, ci: (0, 0, 0)),
                  full2((1, d_rnn)), full2((1, d_rnn)), full2((1, d_rnn))],
        out_specs=chunk,
        out_shape=jax.ShapeDtypeStruct((b, s, d_rnn), BF16),
        scratch_shapes=[pltpu.VMEM((SUBLANES, d_rnn), F32),
                        pltpu.VMEM(slab, F32),
                        pltpu.VMEM(slab, F32),
                        pltpu.VMEM((tc, d_rnn), F32),
                        pltpu.VMEM((tc, d_rnn), BF16),
                        pltpu.VMEM((SUBLANES, d_rnn), F32)],
        compiler_params=_cparams("parallel", "arbitrary"),
        name="rnn",
    )(xr, gg, conv_w, row(conv_b), wband, row(b_a), row(b_x), row(lam))


PRUNE_LOG = -104.0
SOFTPLUS_LINEAR = 80.0


def _attn_kernel(q_ref, k_ref, v_ref, tri_ref, o_ref, *, tq, hp):
    qi = pl.program_id(2)
    row = lax.broadcasted_iota(jnp.int32, (tq, tq), 0)
    col = lax.broadcasted_iota(jnp.int32, (tq, tq), 1)
    causal = col < row

    def sweep(last_kb, n_blocks, diag, state):
        rows = pl.ds(pl.multiple_of((last_kb - (n_blocks - 1)) * tq, tq), n_blocks * tq)
        zs, sps = [], []
        for hd in range(hp):
            lanes = slice(hd * HEAD_DIM, (hd + 1) * HEAD_DIM)
            z = lax.dot_general(q_ref[:, lanes], k_ref[rows, lanes], (((1,), (1,)), ((), ())),
                                preferred_element_type=F32)
            for blk in range(n_blocks):
                zb = z[:, blk * tq:(blk + 1) * tq]
                sp = jnp.where(zb > SOFTPLUS_LINEAR, zb, jnp.log(1.0 + jnp.exp(zb)))
                if diag and blk == n_blocks - 1:
                    sp = jnp.where(causal, sp, 0.0)
                zs.append(zb)
                sps.append(sp)
        excl = jnp.dot(jnp.concatenate([sp.astype(BF16) for sp in sps], axis=0), tri_ref[...],
                       preferred_element_type=F32)
        out = []
        for hd in range(hp):
            lanes = slice(hd * HEAD_DIM, (hd + 1) * HEAD_DIM)
            run, acc = state[hd]
            ws = [None] * n_blocks
            for blk in reversed(range(n_blocks)):
                u = hd * n_blocks + blk
                w = jnp.exp((zs[u] - sps[u]) + excl[u * tq:(u + 1) * tq] + run)
                if diag and blk == n_blocks - 1:
                    w = jnp.where(causal, w, 0.0)
                ws[blk] = w.astype(BF16)
                run = run - jnp.sum(sps[u], axis=-1, keepdims=True)
            acc = acc + jnp.dot(jnp.concatenate(ws, axis=1), v_ref[rows, lanes], preferred_element_type=F32)
            out.append((run, acc))
        return tuple(out)

    def run_max(state):
        m = jnp.max(state[0][0])
        for run, _ in state[1:]:
            m = jnp.maximum(m, jnp.max(run))
        return m

    zero = tuple((jnp.zeros((tq, 1), F32), jnp.zeros((tq, HEAD_DIM), F32)) for _ in range(hp))

    def store(state):
        for hd, (_, acc) in enumerate(state):
            o_ref[:, hd * HEAD_DIM:(hd + 1) * HEAD_DIM] = acc.astype(o_ref.dtype)

    @pl.when(qi == 0)
    def _():
        store(sweep(qi, 1, True, zero))

    @pl.when(qi > 0)
    def _():
        state = sweep(qi, 2, True, zero)

        def cond(c):
            kb, rmax, _ = c
            return jnp.logical_and(kb >= 0, rmax > PRUNE_LOG)

        def body(c):
            kb, _, st = c
            st = sweep(kb, 1, False, st)
            return kb - 1, run_max(st), st

        _, _, state = lax.while_loop(cond, body, (qi - 2, run_max(state), state))
        store(state)


def _attention(qk, v, tq=256, hp=8):
    b, s, _ = v.shape
    tq = _tile(s, tq)
    hw = hp * HEAD_DIM
    n_hg = N_HEADS // hp
    tri = jnp.asarray(-np.tril(np.ones((tq, tq), np.float32), -1), BF16)
    return pl.pallas_call(
        functools.partial(_attn_kernel, tq=tq, hp=hp),
        grid=(b, n_hg, s // tq),
        in_specs=[pl.BlockSpec((None, tq, hw), lambda bi, h, qi: (bi, qi, h)),
                  pl.BlockSpec((None, s, hw), lambda bi, h, qi: (bi, 0, n_hg + h)),
                  pl.BlockSpec((None, s, hw), lambda bi, h, qi: (bi, 0, h)),
                  pl.BlockSpec((tq, tq), lambda bi, h, qi: (0, 0))],
        out_specs=pl.BlockSpec((None, tq, hw), lambda bi, h, qi: (bi, qi, h)),
        out_shape=jax.ShapeDtypeStruct(v.shape, BF16),
        compiler_params=_cparams("parallel", "parallel", "arbitrary"),
        name="attention",
    )(qk, qk, v, tri)


def _mix_kernel(yr_ref, oa_ref, wr_ref, wa_ref, sr_ref, sa_ref, o_ref):
    ya = jnp.dot(yr_ref[...], wr_ref[...], preferred_element_type=F32)
    yb = jnp.dot(oa_ref[...], wa_ref[...], preferred_element_type=F32)
    o_ref[...] = (sr_ref[...] * ya + sa_ref[...] * yb).astype(o_ref.dtype)


def _mix(yr, oa, w_r, w_a, gates, tm=1024, tn=1024):
    t, d_rnn = yr.shape
    d_attn = oa.shape[1]
    n = w_r.shape[1]
    tm = _tile(t, tm)
    tn = _tile(n, tn)
    nj = n // tn
    return pl.pallas_call(
        _mix_kernel,
        grid=(nj, t // tm),
        in_specs=[pl.BlockSpec((tm, d_rnn), lambda j, i: (i, 0)),
                  pl.BlockSpec((tm, d_attn), lambda j, i: (i, 0)),
                  pl.BlockSpec((d_rnn, tn), lambda j, i: (0, j), pipeline_mode=pl.Buffered(1)),
                  pl.BlockSpec((d_attn, tn), lambda j, i: (0, j), pipeline_mode=pl.Buffered(1)),
                  pl.BlockSpec((tm, tn), lambda j, i: (i, j)),
                  pl.BlockSpec((tm, tn), lambda j, i: (i, nj + j))],
        out_specs=pl.BlockSpec((tm, tn), lambda j, i: (i, j)),
        out_shape=jax.ShapeDtypeStruct((t, n), BF16),
        compiler_params=_cparams("parallel", "parallel"),
        name="mix",
    )(yr, oa, w_r, w_a, gates, gates)


def _oproj_kernel(m_ref, w_ref, x_ref, g_ref, x1_ref, h_ref):
    x1 = x_ref[...] + jnp.dot(m_ref[...], w_ref[...], preferred_element_type=F32)
    x1_ref[...] = x1
    ms = jnp.mean(x1 * x1, axis=-1, keepdims=True)
    h_ref[...] = (x1 * lax.rsqrt(ms + EPS) * g_ref[...]).astype(h_ref.dtype)


def _oproj(mix, w_o, x, g_next, tm=512):
    t, d = x.shape
    tm = _tile(t, tm)
    return pl.pallas_call(
        _oproj_kernel,
        grid=(t // tm,),
        in_specs=[pl.BlockSpec((tm, d), lambda i: (i, 0)),
                  pl.BlockSpec((d, d), lambda i: (0, 0)),
                  pl.BlockSpec((tm, d), lambda i: (i, 0)),
                  pl.BlockSpec((1, d), lambda i: (0, 0))],
        out_specs=[pl.BlockSpec((tm, d), lambda i: (i, 0)),
                   pl.BlockSpec((tm, d), lambda i: (i, 0))],
        out_shape=[jax.ShapeDtypeStruct((t, d), F32), jax.ShapeDtypeStruct((t, d), BF16)],
        compiler_params=_cparams("parallel"),
        name="oproj",
    )(mix, w_o, x, g_next.reshape(1, d))


FFN_VMEM_LIMIT = 60 * 1024 * 1024


def _ffn_kernel(h_ref, wg_ref, wu_ref, wd_ref, x_ref, g_ref, x2_ref, h3_ref):
    f = pl.program_id(1)

    @pl.when(f == 0)
    def _():
        x2_ref[...] = x_ref[...]

    h = h_ref[...]
    g = jnp.dot(h, wg_ref[...], preferred_element_type=F32)
    u = jnp.dot(h, wu_ref[...], preferred_element_type=F32)
    act = (jax.nn.silu(g) * u).astype(BF16)
    x2_ref[...] += jnp.dot(act, wd_ref[...], preferred_element_type=F32)

    @pl.when(f == pl.num_programs(1) - 1)
    def _():
        x2 = x2_ref[...]
        ms = jnp.mean(x2 * x2, axis=-1, keepdims=True)
        h3_ref[...] = (x2 * lax.rsqrt(ms + EPS) * g_ref[...]).astype(h3_ref.dtype)


def _ffn(h2, w_gu, w_d, x1, g_next, tm=1024, tf=512):
    t, d = x1.shape
    d_ff = w_d.shape[0]
    tm = _tile(t, tm)
    tf = _tile(d_ff, tf)
    nf = d_ff // tf
    once = dict(pipeline_mode=pl.Buffered(1))
    return pl.pallas_call(
        _ffn_kernel,
        grid=(t // tm, d_ff // tf),
        in_specs=[pl.BlockSpec((tm, d), lambda i, f: (i, 0)),
                  pl.BlockSpec((d, tf), lambda i, f: (0, f)),
                  pl.BlockSpec((d, tf), lambda i, f: (0, nf + f)),
                  pl.BlockSpec((tf, d), lambda i, f: (f, 0)),
                  pl.BlockSpec((tm, d), lambda i, f: (i, 0)),
                  pl.BlockSpec((1, d), lambda i, f: (0, 0))],
        out_specs=[pl.BlockSpec((tm, d), lambda i, f: (i, 0), **once),
                   pl.BlockSpec((tm, d), lambda i, f: (i, 0), **once)],
        out_shape=[jax.ShapeDtypeStruct((t, d), F32), jax.ShapeDtypeStruct((t, d), BF16)],
        compiler_params=pltpu.CompilerParams(dimension_semantics=("parallel", "arbitrary"),
                                             vmem_limit_bytes=FFN_VMEM_LIMIT),
        name="ffn",
    )(h2, w_gu, w_gu, w_d, x1, g_next.reshape(1, d))


def _ple_kernel(h_ref, wg_ref, p_ref, wp_ref, x_ref, o_ref):
    rows = h_ref.shape[0] // PROJ_ROW_SLABS
    for m in range(PROJ_ROW_SLABS):
        sl = slice(m * rows, (m + 1) * rows)
        gate = jax.nn.sigmoid(jnp.dot(h_ref[sl, :], wg_ref[...], preferred_element_type=F32))
        emb = jnp.dot(p_ref[sl, :], wp_ref[...], preferred_element_type=F32)
        o_ref[sl, :] = x_ref[sl, :] + gate * emb


def _ple(h3, w_gate, p, w_proj, x2, tm=1024, tn=1024):
    t, d = x2.shape
    dp = p.shape[1]
    tm = _tile(t, tm)
    tn = _tile(d, tn)
    return pl.pallas_call(
        _ple_kernel,
        grid=(t // tm, d // tn),
        in_specs=[pl.BlockSpec((tm, d), lambda i, j: (i, 0)),
                  pl.BlockSpec((d, tn), lambda i, j: (0, j)),
                  pl.BlockSpec((tm, dp), lambda i, j: (i, 0)),
                  pl.BlockSpec((dp, tn), lambda i, j: (0, j)),
                  pl.BlockSpec((tm, tn), lambda i, j: (i, j))],
        out_specs=pl.BlockSpec((tm, tn), lambda i, j: (i, j)),
        out_shape=jax.ShapeDtypeStruct((t, d), F32),
        compiler_params=_cparams("parallel", "parallel"),
        name="ple",
    )(h3, w_gate, p, w_proj, x2)


def _layer(x, p, g_mix, w_in, conv_w, conv_b, w_rg_a, b_rg_a, w_rg_x, b_rg_x, lru_lambda, q_gain,
           k_gain, w_rnn_out, w_attn_out, w_o, g_ffn, w_ffn_gu, w_ffn_down, g_ple, w_ple_gate,
           w_ple_proj):
    b, s, d = x.shape
    t = b * s
    d_rnn = w_rnn_out.shape[0]
    d_attn = w_attn_out.shape[0]
    x2d = x.reshape(t, d)

    o0 = 0
    o1 = o0 + d_rnn
    o2 = o1 + d_rnn
    o3 = o2 + 2 * d_attn
    o4 = o3 + d_attn
    o5 = o4 + 2 * d
    assert o5 == w_in.shape[1]

    h = _rmsnorm(x2d, g_mix)
    xr, = _proj(h, w_in, o0, d_rnn, "none", F32, tn=d_rnn // 2)
    gg, = _proj(h, w_in, o1, d_rnn, "gelu", F32, tn=d_rnn // 2)
    qk_gain = jnp.concatenate([jnp.tile(q_gain * (1.0 / math.sqrt(HEAD_DIM)), N_HEADS),
                               jnp.tile(k_gain, N_HEADS)]).reshape(1, 2 * d_attn)
    qk, w_gu_b, w_attn_out_b = _proj(h, w_in, o2, 2 * d_attn, "headnorm", BF16, gain=qk_gain,
                                     casts=(w_ffn_gu, w_attn_out))
    v, w_rnn_out_b = _proj(h, w_in, o3, d_attn, "none", BF16, casts=(w_rnn_out,))
    gates, w_down_b, w_o_b, w_ple_gate_b = _proj(h, w_in, o4, 2 * d, "sigmoid", BF16,
                                                 casts=(w_ffn_down, w_o, w_ple_gate))

    wband = _band_gate_weights(w_rg_a, w_rg_x)
    yr = _rnn(xr.reshape(b, s, d_rnn), gg.reshape(b, s, d_rnn), conv_w, conv_b, wband,
              b_rg_a, b_rg_x, lru_lambda)
    oa = _attention(qk.reshape(b, s, 2 * d_attn), v.reshape(b, s, d_attn))

    mix = _mix(yr.reshape(t, d_rnn), oa.reshape(t, d_attn), w_rnn_out_b, w_attn_out_b, gates)
    x1, h2 = _oproj(mix, w_o_b, x2d, g_ffn)
    x2, h3 = _ffn(h2, w_gu_b, w_down_b, x1, g_ple)
    out = _ple(h3, w_ple_gate_b, p.reshape(t, -1).astype(BF16), w_ple_proj.astype(BF16), x2)
    return out.reshape(b, s, d)


def kernel(x, p, g_mix, w_in, conv_w, conv_b, w_rg_a, b_rg_a, w_rg_x, b_rg_x, lru_lambda, q_gain,
           k_gain, w_rnn_out, w_attn_out, w_o, g_ffn, w_ffn_gu, w_ffn_down, g_ple, w_ple_gate,
           w_ple_proj):
    params = (g_mix, w_in, conv_w, conv_b, w_rg_a, b_rg_a, w_rg_x, b_rg_x, lru_lambda, q_gain,
              k_gain, w_rnn_out, w_attn_out, w_o, g_ffn, w_ffn_gu, w_ffn_down, g_ple, w_ple_gate,
              w_ple_proj)
    for i in range(p.shape[0]):
        x = _layer(x, p[i], *[a[i] for a in params])
    return x
```

```python
import functools
import math

import jax
import jax.numpy as jnp
import numpy as np
from jax import lax
from jax.experimental import pallas as pl
from jax.experimental.pallas import tpu as pltpu

F32 = jnp.float32
BF16 = jnp.bfloat16

EPS = 1e-6
LRU_C = 8.0
N_HEADS = 16
HEAD_DIM = 128
N_RNN_BLOCKS = 16
CONV_WIDTH = 4

LANES = 128
SUBLANES = 8
GATE_WINDOW = 4 * LANES
VMEM_LIMIT = 56 * 1024 * 1024
F32_TINY = float(np.finfo(np.float32).tiny)


def _cparams(*sem):
    return pltpu.CompilerParams(dimension_semantics=sem, vmem_limit_bytes=VMEM_LIMIT)


def _tile(n, pref):
    t = min(n, pref)
    assert n % t == 0, (n, t)
    return t


def _rmsnorm_kernel(x_ref, g_ref, o_ref):
    x = x_ref[...]
    ms = jnp.mean(x * x, axis=-1, keepdims=True)
    o_ref[...] = (x * lax.rsqrt(ms + EPS) * g_ref[...]).astype(o_ref.dtype)


def _rmsnorm(x, g, tm=512):
    t, d = x.shape
    tm = _tile(t, tm)
    return pl.pallas_call(
        _rmsnorm_kernel,
        grid=(t // tm,),
        in_specs=[pl.BlockSpec((tm, d), lambda i: (i, 0)),
                  pl.BlockSpec((1, d), lambda i: (0, 0))],
        out_specs=pl.BlockSpec((tm, d), lambda i: (i, 0)),
        out_shape=jax.ShapeDtypeStruct((t, d), BF16),
        compiler_params=_cparams("parallel"),
        name="rmsnorm",
    )(x, g.reshape(1, d))


def _head_norm(acc, gain):
    outs = []
    for c in range(acc.shape[1] // HEAD_DIM):
        blk = acc[:, c * HEAD_DIM:(c + 1) * HEAD_DIM]
        ms = jnp.mean(blk * blk, axis=-1, keepdims=True)
        outs.append(blk * lax.rsqrt(ms + EPS) * gain[:, c * HEAD_DIM:(c + 1) * HEAD_DIM])
    return jnp.concatenate(outs, axis=1)


PROJ_ROW_SLABS = 4


def _proj_kernel(h_ref, w_ref, *rest, kind, n_cast):
    wb_ref = rest[-1]
    cast_out = rest[len(rest) - 1 - n_cast:-1]
    o_ref = rest[-2 - n_cast]
    cast_in = rest[-2 - 2 * n_cast:-2 - n_cast]
    extra = rest[:-2 - 2 * n_cast]

    @pl.when(pl.program_id(1) == 0)
    def _():
        wb_ref[...] = w_ref[...].astype(BF16)

    rows = h_ref.shape[0] // PROJ_ROW_SLABS
    for m in range(PROJ_ROW_SLABS):
        sl = slice(m * rows, (m + 1) * rows)
        acc = jnp.dot(h_ref[sl, :], wb_ref[...], preferred_element_type=F32)
        if kind == "gelu":
            acc = jax.nn.gelu(acc)
        elif kind == "sigmoid":
            acc = jax.nn.sigmoid(acc)
        elif kind == "headnorm":
            acc = _head_norm(acc, extra[0][...])
        o_ref[sl, :] = acc.astype(o_ref.dtype)

    for src_ref, dst_ref in zip(cast_in, cast_out):
        dst_ref[...] = src_ref[...].astype(BF16)


def _proj(h, w, col0, n, kind, out_dtype, gain=None, casts=(), tm=1024, tn=1024):
    t, d = h.shape
    tm = _tile(t, tm)
    tn = _tile(n, tn)
    assert col0 % LANES == 0 and tn % LANES == 0 and tm % (PROJ_ROW_SLABS * 2 * SUBLANES) == 0
    n_i = t // tm
    n_steps = (n // tn) * n_i
    in_specs = [pl.BlockSpec((tm, d), lambda j, i: (i, 0)),
                pl.BlockSpec((pl.Element(d), pl.Element(tn)),
                             lambda j, i: (0, pl.multiple_of(col0 + j * tn, LANES)))]
    args = [h, w]
    if gain is not None:
        in_specs.append(pl.BlockSpec((1, tn), lambda j, i: (0, j)))
        args.append(gain)
    out_specs = [pl.BlockSpec((tm, tn), lambda j, i: (i, j))]
    out_shape = [jax.ShapeDtypeStruct((t, n), out_dtype)]
    for c in casts:
        rows = c.shape[0] // n_steps
        assert rows * n_steps == c.shape[0] and rows % (2 * SUBLANES) == 0, (c.shape, n_steps)
        spec = pl.BlockSpec((rows, c.shape[1]), lambda j, i: (j * n_i + i, 0))
        in_specs.append(spec)
        args.append(c)
        out_specs.append(spec)
        out_shape.append(jax.ShapeDtypeStruct(c.shape, BF16))
    return pl.pallas_call(
        functools.partial(_proj_kernel, kind=kind, n_cast=len(casts)),
        grid=(n // tn, n_i),
        in_specs=in_specs,
        out_specs=out_specs,
        out_shape=out_shape,
        scratch_shapes=[pltpu.VMEM((d, tn), BF16)],
        compiler_params=_cparams("parallel", "arbitrary"),
        name="proj_" + kind,
    )(*args)


BIG_VMEM_LIMIT = 60 * 1024 * 1024


def _proj_rnn_inputs_kernel(h_ref, w_ref, o_ref, wb_ref, *, row_slabs):
    @pl.when(pl.program_id(1) == 0)
    def _():
        wb_ref[...] = w_ref[...].astype(BF16)

    rows = h_ref.shape[0] // row_slabs

    def body(gelu):
        for m in range(row_slabs):
            sl = slice(m * rows, (m + 1) * rows)
            acc = jnp.dot(h_ref[sl, :], wb_ref[...], preferred_element_type=F32)
            o_ref[sl, :] = jax.nn.gelu(acc) if gelu else acc

    @pl.when(pl.program_id(0) == 0)
    def _():
        body(False)

    @pl.when(pl.program_id(0) == 1)
    def _():
        body(True)


def _proj_rnn_inputs(h, w, col0, d_rnn, tm=512, row_slabs=2):
    t, d = h.shape
    tm = _tile(t, tm)
    assert col0 % LANES == 0 and d_rnn % LANES == 0 and tm % (row_slabs * 2 * SUBLANES) == 0
    return pl.pallas_call(
        functools.partial(_proj_rnn_inputs_kernel, row_slabs=row_slabs),
        grid=(2, t // tm),
        in_specs=[pl.BlockSpec((tm, d), lambda j, i: (i, 0)),
                  pl.BlockSpec((pl.Element(d), pl.Element(d_rnn)),
                               lambda j, i: (0, pl.multiple_of(col0 + j * d_rnn, LANES)),
                               pipeline_mode=pl.Buffered(1))],
        out_specs=pl.BlockSpec((tm, d_rnn), lambda j, i: (i, j)),
        out_shape=jax.ShapeDtypeStruct((t, 2 * d_rnn), F32),
        scratch_shapes=[pltpu.VMEM((d, d_rnn), BF16)],
        compiler_params=pltpu.CompilerParams(dimension_semantics=("arbitrary", "arbitrary"),
                                             vmem_limit_bytes=BIG_VMEM_LIMIT),
        name="proj_rnn_inputs",
    )(h, w)


def _gate_window_starts(d_rnn):
    blk = d_rnn // N_RNN_BLOCKS
    n_tiles = d_rnn // LANES
    starts = []
    for j in range(n_tiles):
        b0 = (LANES * j) // blk
        b1 = (LANES * j + LANES - 1) // blk
        s = min((blk * b0) // LANES, n_tiles - GATE_WINDOW // LANES)
        assert s * LANES <= blk * b0 and blk * (b1 + 1) <= s * LANES + GATE_WINDOW
        starts.append(s)
    return starts


def _band_gate_weights(w_a, w_x):
    nb, blk, _ = w_a.shape
    d_rnn = nb * blk
    starts = _gate_window_starts(d_rnn)

    def band(w):
        tiles = []
        for j, s in enumerate(starts):
            c_lo, c_hi = LANES * j, LANES * (j + 1)
            pieces = []
            for b in range(c_lo // blk, (c_hi - 1) // blk + 1):
                oc0, oc1 = max(c_lo, blk * b), min(c_hi, blk * (b + 1))
                r0 = blk * b - s * LANES
                pieces.append(jnp.pad(w[b, :, oc0 - blk * b:oc1 - blk * b],
                                      ((r0, GATE_WINDOW - blk - r0), (0, 0))))
            tiles.append(jnp.concatenate(pieces, axis=1))
        return jnp.stack(tiles)

    return jnp.concatenate([band(w_a), band(w_x)], axis=-1).astype(BF16)


def _rnn_kernel(xr_ref, gg_ref, cw_ref, cb_ref, wband_ref, ba_ref, bx_ref, lam_ref, y_ref,
                tail_ref, xin_ref, hout_ref, xc_ref, xcb_ref, hcar_ref, *, tc, d_rnn, starts):
    n_tiles = d_rnn // LANES
    seg = tc // SUBLANES
    pitch = seg + SUBLANES

    @pl.when(pl.program_id(1) == 0)
    def _():
        tail_ref[...] = jnp.zeros_like(tail_ref)
        hcar_ref[...] = jnp.zeros_like(hcar_ref)

    sub = lax.broadcasted_iota(jnp.int32, (SUBLANES, LANES), 0)

    def seg_rows(p):
        return slice(p * seg, (p + 1) * seg)

    def slab_rows(p):
        return slice(p * pitch, p * pitch + seg)

    for j in range(n_tiles):
        cols = slice(j * LANES, (j + 1) * LANES)
        for p in range(SUBLANES):
            xin_ref[j, slab_rows(p), :] = xr_ref[seg_rows(p), cols]
        x = [xin_ref[j, pl.ds(t, SUBLANES, stride=pitch), :] for t in range(seg)]
        half_w = [0.5 * cw_ref[k:k + 1, cols] for k in range(CONV_WIDTH)]
        half_b = 0.5 * cb_ref[:, cols]

        def shifted(t, s):
            if t >= s:
                return x[t - s]
            i = SUBLANES - s + t
            prev = jnp.broadcast_to(tail_ref[i:i + 1, cols], (SUBLANES, LANES))
            return jnp.where(sub >= 1, pltpu.roll(x[seg + t - s], 1, axis=0), prev)

        xc = []
        for t in range(seg):
            acc = half_b + half_w[CONV_WIDTH - 1] * x[t]
            for s in range(1, CONV_WIDTH):
                acc = acc + half_w[CONV_WIDTH - 1 - s] * shifted(t, s)
            xc.append(acc)
        xc = jnp.concatenate(xc, axis=0)
        xc_ref[:, cols] = xc
        xcb_ref[:, cols] = xc.astype(BF16)

    tail_ref[...] = xr_ref[tc - SUBLANES:tc, :]

    for j in range(n_tiles):
        cols = slice(j * LANES, (j + 1) * LANES)
        ks = starts[j] * LANES
        pre = jnp.dot(xcb_ref[:, ks:ks + GATE_WINDOW], wband_ref[j], preferred_element_type=F32)
        tanh_r = jnp.tanh(pre[:, :LANES] + 0.5 * ba_ref[:, cols])
        tanh_i = jnp.tanh(pre[:, LANES:] + 0.5 * bx_ref[:, cols])
        nlam = -lam_ref[:, cols]
        softplus = jnp.maximum(nlam, 0.0) + jnp.log1p(jnp.exp(-jnp.abs(nlam)))
        half_c = (-0.5 * LRU_C) * softplus
        log_a = half_c * tanh_r + half_c
        a = jnp.exp(log_a)
        y = -jnp.tanh(log_a) * (a * a + 1.0)
        gated_x = (tanh_i + 1.0) * xc_ref[:, cols]
        u = y * lax.rsqrt(jnp.maximum(y, F32_TINY)) * gated_x
        a = a.reshape(seg, SUBLANES, LANES)
        u = u.reshape(seg, SUBLANES, LANES)

        h = u[0]
        decay = a[0]
        for t in range(1, seg):
            h = a[t] * h + u[t]
            decay = decay * a[t]
        for d in (1, 2, 4):
            m = sub >= d
            h = h + decay * jnp.where(m, pltpu.roll(h, d, axis=0), 0.0)
            decay = decay * jnp.where(m, pltpu.roll(decay, d, axis=0), 1.0)
        h_in = hcar_ref[:, cols]
        h_end = h + decay * h_in
        hcar_ref[:, cols] = jnp.broadcast_to(h_end[SUBLANES - 1:SUBLANES, :], (SUBLANES, LANES))
        h = jnp.where(sub >= 1, pltpu.roll(h_end, 1, axis=0), h_in)
        for t in range(seg):
            h = a[t] * h + u[t]
            hout_ref[j, pl.ds(t, SUBLANES, stride=pitch), :] = h
        for p in range(SUBLANES):
            y_ref[seg_rows(p), cols] = (gg_ref[seg_rows(p), cols] * hout_ref[j, slab_rows(p), :]).astype(y_ref.dtype)


def _rnn(xr_gg, conv_w, conv_b, wband, b_a, b_x, lam, tc=256):
    b, s, d_rnn = xr_gg.shape[0], xr_gg.shape[1], xr_gg.shape[2] // 2
    tc = _tile(s, tc)
    starts = _gate_window_starts(d_rnn)
    n_tiles = d_rnn // LANES
    slab = (n_tiles, tc + SUBLANES * SUBLANES, LANES)
    row = lambda v: v.reshape(1, d_rnn)
    full2 = lambda shape: pl.BlockSpec(shape, lambda bi, ci: (0, 0))
    chunk = pl.BlockSpec((None, tc, d_rnn), lambda bi, ci: (bi, ci, 0))
    gate_chunk = pl.BlockSpec((None, tc, d_rnn), lambda bi, ci: (bi, ci, 1))
    return pl.pallas_call(
        functools.partial(_rnn_kernel, tc=tc, d_rnn=d_rnn, starts=starts),
        grid=(b, s // tc),
        in_specs=[chunk, gate_chunk,
                  full2((CONV_WIDTH, d_rnn)), full2((1, d_rnn)),
                  pl.BlockSpec(wband.shape, lambda bi, ci: (0, 0, 0)),
                  full2((1, d_rnn)), full2((1, d_rnn)), full2((1, d_rnn))],
        out_specs=chunk,
        out_shape=jax.ShapeDtypeStruct((b, s, d_rnn), BF16),
        scratch_shapes=[pltpu.VMEM((SUBLANES, d_rnn), F32),
                        pltpu.VMEM(slab, F32),
                        pltpu.VMEM(slab, F32),
                        pltpu.VMEM((tc, d_rnn), F32),
                        pltpu.VMEM((tc, d_rnn), BF16),
                        pltpu.VMEM((SUBLANES, d_rnn), F32)],
        compiler_params=_cparams("parallel", "arbitrary"),
        name="rnn",
    )(xr_gg, xr_gg, conv_w, row(conv_b), wband, row(b_a), row(b_x), row(lam))


PRUNE_LOG = -104.0
SOFTPLUS_LINEAR = 80.0


def _attn_kernel(q_ref, k_ref, v_ref, tri_ref, o_ref, *, tq, hp):
    qi = pl.program_id(2)
    row = lax.broadcasted_iota(jnp.int32, (tq, tq), 0)
    col = lax.broadcasted_iota(jnp.int32, (tq, tq), 1)
    causal = col < row

    def sweep(last_kb, n_blocks, diag, state):
        rows = pl.ds(pl.multiple_of((last_kb - (n_blocks - 1)) * tq, tq), n_blocks * tq)
        zs, sps = [], []
        for hd in range(hp):
            lanes = slice(hd * HEAD_DIM, (hd + 1) * HEAD_DIM)
            z = lax.dot_general(q_ref[:, lanes], k_ref[rows, lanes], (((1,), (1,)), ((), ())),
                                preferred_element_type=F32)
            for blk in range(n_blocks):
                zb = z[:, blk * tq:(blk + 1) * tq]
                sp = jnp.where(zb > SOFTPLUS_LINEAR, zb, jnp.log(1.0 + jnp.exp(zb)))
                if diag and blk == n_blocks - 1:
                    sp = jnp.where(causal, sp, 0.0)
                zs.append(zb)
                sps.append(sp)
        excl = jnp.dot(jnp.concatenate([sp.astype(BF16) for sp in sps], axis=0), tri_ref[...],
                       preferred_element_type=F32)
        out = []
        for hd in range(hp):
            lanes = slice(hd * HEAD_DIM, (hd + 1) * HEAD_DIM)
            run, acc = state[hd]
            ws = [None] * n_blocks
            for blk in reversed(range(n_blocks)):
                u = hd * n_blocks + blk
                w = jnp.exp((zs[u] - sps[u]) + excl[u * tq:(u + 1) * tq] + run)
                if diag and blk == n_blocks - 1:
                    w = jnp.where(causal, w, 0.0)
                ws[blk] = w.astype(BF16)
                run = run - jnp.sum(sps[u], axis=-1, keepdims=True)
            acc = acc + jnp.dot(jnp.concatenate(ws, axis=1), v_ref[rows, lanes], preferred_element_type=F32)
            out.append((run, acc))
        return tuple(out)

    def run_max(state):
        m = jnp.max(state[0][0])
        for run, _ in state[1:]:
            m = jnp.maximum(m, jnp.max(run))
        return m

    zero = tuple((jnp.zeros((tq, 1), F32), jnp.zeros((tq, HEAD_DIM), F32)) for _ in range(hp))

    def store(state):
        for hd, (_, acc) in enumerate(state):
            o_ref[:, hd * HEAD_DIM:(hd + 1) * HEAD_DIM] = acc.astype(o_ref.dtype)

    @pl.when(qi == 0)
    def _():
        store(sweep(qi, 1, True, zero))

    @pl.when(qi > 0)
    def _():
        state = sweep(qi, 2, True, zero)

        def cond(c):
            kb, rmax, _ = c
            return jnp.logical_and(kb >= 0, rmax > PRUNE_LOG)

        def body(c):
            kb, _, st = c
            st = sweep(kb, 1, False, st)
            return kb - 1, run_max(st), st

        _, _, state = lax.while_loop(cond, body, (qi - 2, run_max(state), state))
        store(state)


def _attention(qk, v, tq=256, hp=8):
    b, s, _ = v.shape
    tq = _tile(s, tq)
    hw = hp * HEAD_DIM
    n_hg = N_HEADS // hp
    tri = jnp.asarray(-np.tril(np.ones((tq, tq), np.float32), -1), BF16)
    return pl.pallas_call(
        functools.partial(_attn_kernel, tq=tq, hp=hp),
        grid=(b, n_hg, s // tq),
        in_specs=[pl.BlockSpec((None, tq, hw), lambda bi, h, qi: (bi, qi, h)),
                  pl.BlockSpec((None, s, hw), lambda bi, h, qi: (bi, 0, n_hg + h)),
                  pl.BlockSpec((None, s, hw), lambda bi, h, qi: (bi, 0, h)),
                  pl.BlockSpec((tq, tq), lambda bi, h, qi: (0, 0))],
        out_specs=pl.BlockSpec((None, tq, hw), lambda bi, h, qi: (bi, qi, h)),
        out_shape=jax.ShapeDtypeStruct(v.shape, BF16),
        compiler_params=_cparams("parallel", "parallel", "arbitrary"),
        name="attention",
    )(qk, qk, v, tri)


def _mix_kernel(yr_ref, oa_ref, wr_ref, wa_ref, sr_ref, sa_ref, o_ref):
    ya = jnp.dot(yr_ref[...], wr_ref[...], preferred_element_type=F32)
    yb = jnp.dot(oa_ref[...], wa_ref[...], preferred_element_type=F32)
    o_ref[...] = (sr_ref[...] * ya + sa_ref[...] * yb).astype(o_ref.dtype)


def _mix(yr, oa, w_r, w_a, gates, tm=1024, tn=1024):
    t, d_rnn = yr.shape
    d_attn = oa.shape[1]
    n = w_r.shape[1]
    tm = _tile(t, tm)
    tn = _tile(n, tn)
    nj = n // tn
    return pl.pallas_call(
        _mix_kernel,
        grid=(nj, t // tm),
        in_specs=[pl.BlockSpec((tm, d_rnn), lambda j, i: (i, 0)),
                  pl.BlockSpec((tm, d_attn), lambda j, i: (i, 0)),
                  pl.BlockSpec((d_rnn, tn), lambda j, i: (0, j), pipeline_mode=pl.Buffered(1)),
                  pl.BlockSpec((d_attn, tn), lambda j, i: (0, j), pipeline_mode=pl.Buffered(1)),
                  pl.BlockSpec((tm, tn), lambda j, i: (i, j)),
                  pl.BlockSpec((tm, tn), lambda j, i: (i, nj + j))],
        out_specs=pl.BlockSpec((tm, tn), lambda j, i: (i, j)),
        out_shape=jax.ShapeDtypeStruct((t, n), BF16),
        compiler_params=_cparams("parallel", "parallel"),
        name="mix",
    )(yr, oa, w_r, w_a, gates, gates)


def _oproj_kernel(m_ref, w_ref, x_ref, g_ref, x1_ref, h_ref):
    x1 = x_ref[...] + jnp.dot(m_ref[...], w_ref[...], preferred_element_type=F32)
    x1_ref[...] = x1
    ms = jnp.mean(x1 * x1, axis=-1, keepdims=True)
    h_ref[...] = (x1 * lax.rsqrt(ms + EPS) * g_ref[...]).astype(h_ref.dtype)


def _oproj(mix, w_o, x, g_next, tm=512):
    t, d = x.shape
    tm = _tile(t, tm)
    return pl.pallas_call(
        _oproj_kernel,
        grid=(t // tm,),
        in_specs=[pl.BlockSpec((tm, d), lambda i: (i, 0)),
                  pl.BlockSpec((d, d), lambda i: (0, 0)),
                  pl.BlockSpec((tm, d), lambda i: (i, 0)),
                  pl.BlockSpec((1, d), lambda i: (0, 0))],
        out_specs=[pl.BlockSpec((tm, d), lambda i: (i, 0)),
                   pl.BlockSpec((tm, d), lambda i: (i, 0))],
        out_shape=[jax.ShapeDtypeStruct((t, d), F32), jax.ShapeDtypeStruct((t, d), BF16)],
        compiler_params=_cparams("parallel"),
        name="oproj",
    )(mix, w_o, x, g_next.reshape(1, d))


def _ffn_kernel(h_ref, wg_ref, wu_ref, wd_ref, x_ref, g_ref, x2_ref, h3_ref):
    f = pl.program_id(1)

    @pl.when(f == 0)
    def _():
        x2_ref[...] = x_ref[...]

    h = h_ref[...]
    g = jnp.dot(h, wg_ref[...], preferred_element_type=F32)
    u = jnp.dot(h, wu_ref[...], preferred_element_type=F32)
    act = (jax.nn.silu(g) * u).astype(BF16)
    x2_ref[...] += jnp.dot(act, wd_ref[...], preferred_element_type=F32)

    @pl.when(f == pl.num_programs(1) - 1)
    def _():
        x2 = x2_ref[...]
        ms = jnp.mean(x2 * x2, axis=-1, keepdims=True)
        h3_ref[...] = (x2 * lax.rsqrt(ms + EPS) * g_ref[...]).astype(h3_ref.dtype)


def _ffn(h2, w_gu, w_d, x1, g_next, tm=1024, tf=512):
    t, d = x1.shape
    d_ff = w_d.shape[0]
    tm = _tile(t, tm)
    tf = _tile(d_ff, tf)
    nf = d_ff // tf
    once = dict(pipeline_mode=pl.Buffered(1))
    return pl.pallas_call(
        _ffn_kernel,
        grid=(t // tm, d_ff // tf),
        in_specs=[pl.BlockSpec((tm, d), lambda i, f: (i, 0)),
                  pl.BlockSpec((d, tf), lambda i, f: (0, f)),
                  pl.BlockSpec((d, tf), lambda i, f: (0, nf + f)),
                  pl.BlockSpec((tf, d), lambda i, f: (f, 0)),
                  pl.BlockSpec((tm, d), lambda i, f: (i, 0)),
                  pl.BlockSpec((1, d), lambda i, f: (0, 0))],
        out_specs=[pl.BlockSpec((tm, d), lambda i, f: (i, 0), **once),
                   pl.BlockSpec((tm, d), lambda i, f: (i, 0), **once)],
        out_shape=[jax.ShapeDtypeStruct((t, d), F32), jax.ShapeDtypeStruct((t, d), BF16)],
        compiler_params=pltpu.CompilerParams(dimension_semantics=("parallel", "arbitrary"),
                                             vmem_limit_bytes=BIG_VMEM_LIMIT),
        name="ffn",
    )(h2, w_gu, w_gu, w_d, x1, g_next.reshape(1, d))


def _ple_kernel(h_ref, wg_ref, p_ref, wp_ref, x_ref, o_ref):
    rows = h_ref.shape[0] // PROJ_ROW_SLABS
    for m in range(PROJ_ROW_SLABS):
        sl = slice(m * rows, (m + 1) * rows)
        gate = jax.nn.sigmoid(jnp.dot(h_ref[sl, :], wg_ref[...], preferred_element_type=F32))
        emb = jnp.dot(p_ref[sl, :], wp_ref[...], preferred_element_type=F32)
        o_ref[sl, :] = x_ref[sl, :] + gate * emb


def _ple(h3, w_gate, p, w_proj, x2, tm=1024, tn=1024):
    t, d = x2.shape
    dp = p.shape[1]
    tm = _tile(t, tm)
    tn = _tile(d, tn)
    return pl.pallas_call(
        _ple_kernel,
        grid=(t // tm, d // tn),
        in_specs=[pl.BlockSpec((tm, d), lambda i, j: (i, 0)),
                  pl.BlockSpec((d, tn), lambda i, j: (0, j)),
                  pl.BlockSpec((tm, dp), lambda i, j: (i, 0)),
                  pl.BlockSpec((dp, tn), lambda i, j: (0, j)),
                  pl.BlockSpec((tm, tn), lambda i, j: (i, j))],
        out_specs=pl.BlockSpec((tm, tn), lambda i, j: (i, j)),
        out_shape=jax.ShapeDtypeStruct((t, d), F32),
        compiler_params=_cparams("parallel", "parallel"),
        name="ple",
    )(h3, w_gate, p, w_proj, x2)


def _layer(x, p, g_mix, w_in, conv_w, conv_b, w_rg_a, b_rg_a, w_rg_x, b_rg_x, lru_lambda, q_gain,
           k_gain, w_rnn_out, w_attn_out, w_o, g_ffn, w_ffn_gu, w_ffn_down, g_ple, w_ple_gate,
           w_ple_proj):
    b, s, d = x.shape
    t = b * s
    d_rnn = w_rnn_out.shape[0]
    d_attn = w_attn_out.shape[0]
    x2d = x.reshape(t, d)

    o0 = 0
    o1 = o0 + d_rnn
    o2 = o1 + d_rnn
    o3 = o2 + 2 * d_attn
    o4 = o3 + d_attn
    o5 = o4 + 2 * d
    assert o5 == w_in.shape[1]

    h = _rmsnorm(x2d, g_mix)
    xr_gg = _proj_rnn_inputs(h, w_in, o0, d_rnn)
    qk_gain = jnp.concatenate([jnp.tile(q_gain * (1.0 / math.sqrt(HEAD_DIM)), N_HEADS),
                               jnp.tile(k_gain, N_HEADS)]).reshape(1, 2 * d_attn)
    qk, w_gu_b, w_attn_out_b = _proj(h, w_in, o2, 2 * d_attn, "headnorm", BF16, gain=qk_gain,
                                     casts=(w_ffn_gu, w_attn_out))
    v, w_rnn_out_b = _proj(h, w_in, o3, d_attn, "none", BF16, casts=(w_rnn_out,))
    gates, w_down_b, w_o_b, w_ple_gate_b = _proj(h, w_in, o4, 2 * d, "sigmoid", BF16,
                                                 casts=(w_ffn_down, w_o, w_ple_gate))

    wband = _band_gate_weights(w_rg_a, w_rg_x)
    yr = _rnn(xr_gg.reshape(b, s, 2 * d_rnn), conv_w, conv_b, wband, b_rg_a, b_rg_x, lru_lambda)
    oa = _attention(qk.reshape(b, s, 2 * d_attn), v.reshape(b, s, d_attn))

    mix = _mix(yr.reshape(t, d_rnn), oa.reshape(t, d_attn), w_rnn_out_b, w_attn_out_b, gates)
    x1, h2 = _oproj(mix, w_o_b, x2d, g_ffn)
    x2, h3 = _ffn(h2, w_gu_b, w_down_b, x1, g_ple)
    out = _ple(h3, w_ple_gate_b, p.reshape(t, -1).astype(BF16), w_ple_proj.astype(BF16), x2)
    return out.reshape(b, s, d)


def kernel(x, p, g_mix, w_in, conv_w, conv_b, w_rg_a, b_rg_a, w_rg_x, b_rg_x, lru_lambda, q_gain,
           k_gain, w_rnn_out, w_attn_out, w_o, g_ffn, w_ffn_gu, w_ffn_down, g_ple, w_ple_gate,
           w_ple_proj):
    params = (g_mix, w_in, conv_w, conv_b, w_rg_a, b_rg_a, w_rg_x, b_rg_x, lru_lambda, q_gain,
              k_gain, w_rnn_out, w_attn_out, w_o, g_ffn, w_ffn_gu, w_ffn_down, g_ple, w_ple_gate,
              w_ple_proj)
    for i in range(p.shape[0]):
        x = _layer(x, p[i], *[a[i] for a in params])
    return x
```

```python
import functools
import math

import jax
import jax.numpy as jnp
import numpy as np
from jax import lax
from jax.experimental import pallas as pl
from jax.experimental.pallas import tpu as pltpu

F32 = jnp.float32
BF16 = jnp.bfloat16

EPS = 1e-6
LRU_C = 8.0
N_HEADS = 16
HEAD_DIM = 128
N_RNN_BLOCKS = 16
CONV_WIDTH = 4

LANES = 128
SUBLANES = 8
GATE_WINDOW = 4 * LANES
VMEM_LIMIT = 56 * 1024 * 1024
F32_TINY = float(np.finfo(np.float32).tiny)


def _cparams(*sem):
    return pltpu.CompilerParams(dimension_semantics=sem, vmem_limit_bytes=VMEM_LIMIT)


def _tile(n, pref):
    t = min(n, pref)
    assert n % t == 0, (n, t)
    return t


def _rmsnorm_kernel(x_ref, g_ref, o_ref):
    x = x_ref[...]
    ms = jnp.mean(x * x, axis=-1, keepdims=True)
    o_ref[...] = (x * lax.rsqrt(ms + EPS) * g_ref[...]).astype(o_ref.dtype)


def _rmsnorm(x, g, tm=512):
    t, d = x.shape
    tm = _tile(t, tm)
    return pl.pallas_call(
        _rmsnorm_kernel,
        grid=(t // tm,),
        in_specs=[pl.BlockSpec((tm, d), lambda i: (i, 0)),
                  pl.BlockSpec((1, d), lambda i: (0, 0))],
        out_specs=pl.BlockSpec((tm, d), lambda i: (i, 0)),
        out_shape=jax.ShapeDtypeStruct((t, d), BF16),
        compiler_params=_cparams("parallel"),
        name="rmsnorm",
    )(x, g.reshape(1, d))


def _head_norm(acc, gain):
    outs = []
    for c in range(acc.shape[1] // HEAD_DIM):
        blk = acc[:, c * HEAD_DIM:(c + 1) * HEAD_DIM]
        ms = jnp.mean(blk * blk, axis=-1, keepdims=True)
        outs.append(blk * lax.rsqrt(ms + EPS) * gain[:, c * HEAD_DIM:(c + 1) * HEAD_DIM])
    return jnp.concatenate(outs, axis=1)


PROJ_ROW_SLABS = 4


def _proj_kernel(h_ref, w_ref, *rest, kind, n_cast):
    wb_ref = rest[-1]
    cast_out = rest[len(rest) - 1 - n_cast:-1]
    o_ref = rest[-2 - n_cast]
    cast_in = rest[-2 - 2 * n_cast:-2 - n_cast]
    extra = rest[:-2 - 2 * n_cast]

    @pl.when(pl.program_id(1) == 0)
    def _():
        wb_ref[...] = w_ref[...].astype(BF16)

    rows = h_ref.shape[0] // PROJ_ROW_SLABS
    for m in range(PROJ_ROW_SLABS):
        sl = slice(m * rows, (m + 1) * rows)
        acc = jnp.dot(h_ref[sl, :], wb_ref[...], preferred_element_type=F32)
        if kind == "sigmoid":
            acc = jax.nn.sigmoid(acc)
        elif kind == "headnorm":
            acc = _head_norm(acc, extra[0][...])
        o_ref[sl, :] = acc.astype(o_ref.dtype)

    for src_ref, dst_ref in zip(cast_in, cast_out):
        dst_ref[...] = src_ref[...].astype(BF16)


def _proj(h, w, col0, n, kind, out_dtype, gain=None, casts=(), tm=1024, tn=1024):
    t, d = h.shape
    tm = _tile(t, tm)
    tn = _tile(n, tn)
    assert col0 % LANES == 0 and tn % LANES == 0 and tm % (PROJ_ROW_SLABS * 2 * SUBLANES) == 0
    n_i = t // tm
    n_steps = (n // tn) * n_i
    in_specs = [pl.BlockSpec((tm, d), lambda j, i: (i, 0)),
                pl.BlockSpec((pl.Element(d), pl.Element(tn)),
                             lambda j, i: (0, pl.multiple_of(col0 + j * tn, LANES)))]
    args = [h, w]
    if gain is not None:
        in_specs.append(pl.BlockSpec((1, tn), lambda j, i: (0, j)))
        args.append(gain)
    out_specs = [pl.BlockSpec((tm, tn), lambda j, i: (i, j))]
    out_shape = [jax.ShapeDtypeStruct((t, n), out_dtype)]
    for c in casts:
        rows = c.shape[0] // n_steps
        assert rows * n_steps == c.shape[0] and rows % (2 * SUBLANES) == 0, (c.shape, n_steps)
        spec = pl.BlockSpec((rows, c.shape[1]), lambda j, i: (j * n_i + i, 0))
        in_specs.append(spec)
        args.append(c)
        out_specs.append(spec)
        out_shape.append(jax.ShapeDtypeStruct(c.shape, BF16))
    return pl.pallas_call(
        functools.partial(_proj_kernel, kind=kind, n_cast=len(casts)),
        grid=(n // tn, n_i),
        in_specs=in_specs,
        out_specs=out_specs,
        out_shape=out_shape,
        scratch_shapes=[pltpu.VMEM((d, tn), BF16)],
        compiler_params=_cparams("parallel", "arbitrary"),
        name="proj_" + kind,
    )(*args)


BIG_VMEM_LIMIT = 60 * 1024 * 1024


def _proj_rnn_inputs_kernel(h_ref, w_ref, o_ref, wb_ref, *, row_slabs):
    @pl.when(pl.program_id(1) == 0)
    def _():
        wb_ref[...] = w_ref[...].astype(BF16)

    rows = h_ref.shape[0] // row_slabs

    def body(gelu):
        for m in range(row_slabs):
            sl = slice(m * rows, (m + 1) * rows)
            acc = jnp.dot(h_ref[sl, :], wb_ref[...], preferred_element_type=F32)
            o_ref[sl, :] = jax.nn.gelu(acc) if gelu else acc

    @pl.when(pl.program_id(0) == 0)
    def _():
        body(False)

    @pl.when(pl.program_id(0) == 1)
    def _():
        body(True)


def _proj_rnn_inputs(h, w, col0, d_rnn, tm=512, row_slabs=2):
    t, d = h.shape
    tm = _tile(t, tm)
    assert col0 % LANES == 0 and d_rnn % LANES == 0 and tm % (row_slabs * 2 * SUBLANES) == 0
    return pl.pallas_call(
        functools.partial(_proj_rnn_inputs_kernel, row_slabs=row_slabs),
        grid=(2, t // tm),
        in_specs=[pl.BlockSpec((tm, d), lambda j, i: (i, 0)),
                  pl.BlockSpec((pl.Element(d), pl.Element(d_rnn)),
                               lambda j, i: (0, pl.multiple_of(col0 + j * d_rnn, LANES)),
                               pipeline_mode=pl.Buffered(1))],
        out_specs=pl.BlockSpec((tm, d_rnn), lambda j, i: (i, j)),
        out_shape=jax.ShapeDtypeStruct((t, 2 * d_rnn), F32),
        scratch_shapes=[pltpu.VMEM((d, d_rnn), BF16)],
        compiler_params=pltpu.CompilerParams(dimension_semantics=("arbitrary", "arbitrary"),
                                             vmem_limit_bytes=BIG_VMEM_LIMIT),
        name="proj_rnn_inputs",
    )(h, w)


def _gate_window_starts(d_rnn):
    blk = d_rnn // N_RNN_BLOCKS
    n_tiles = d_rnn // LANES
    starts = []
    for j in range(n_tiles):
        b0 = (LANES * j) // blk
        b1 = (LANES * j + LANES - 1) // blk
        s = min((blk * b0) // LANES, n_tiles - GATE_WINDOW // LANES)
        assert s * LANES <= blk * b0 and blk * (b1 + 1) <= s * LANES + GATE_WINDOW
        starts.append(s)
    return starts


def _band_gate_weights(w_a, w_x):
    nb, blk, _ = w_a.shape
    d_rnn = nb * blk
    starts = _gate_window_starts(d_rnn)

    def band(w):
        tiles = []
        for j, s in enumerate(starts):
            c_lo, c_hi = LANES * j, LANES * (j + 1)
            pieces = []
            for b in range(c_lo // blk, (c_hi - 1) // blk + 1):
                oc0, oc1 = max(c_lo, blk * b), min(c_hi, blk * (b + 1))
                r0 = blk * b - s * LANES
                pieces.append(jnp.pad(w[b, :, oc0 - blk * b:oc1 - blk * b],
                                      ((r0, GATE_WINDOW - blk - r0), (0, 0))))
            tiles.append(jnp.concatenate(pieces, axis=1))
        return jnp.stack(tiles)

    return jnp.concatenate([band(w_a), band(w_x)], axis=-1).astype(BF16)


def _rnn_kernel(xr_ref, gg_ref, cw_ref, cb_ref, wband_ref, ba_ref, bx_ref, lam_ref, y_ref,
                tail_ref, xin_ref, hout_ref, xc_ref, xcb_ref, hcar_ref, *, tc, d_rnn, starts):
    n_tiles = d_rnn // LANES
    seg = tc // SUBLANES
    pitch = seg + SUBLANES

    @pl.when(pl.program_id(1) == 0)
    def _():
        tail_ref[...] = jnp.zeros_like(tail_ref)
        hcar_ref[...] = jnp.zeros_like(hcar_ref)

    sub = lax.broadcasted_iota(jnp.int32, (SUBLANES, LANES), 0)

    def seg_rows(p):
        return slice(p * seg, (p + 1) * seg)

    def slab_rows(p):
        return slice(p * pitch, p * pitch + seg)

    for j in range(n_tiles):
        cols = slice(j * LANES, (j + 1) * LANES)
        for p in range(SUBLANES):
            xin_ref[j, slab_rows(p), :] = xr_ref[seg_rows(p), cols]
        x = [xin_ref[j, pl.ds(t, SUBLANES, stride=pitch), :] for t in range(seg)]
        half_w = [0.5 * cw_ref[k:k + 1, cols] for k in range(CONV_WIDTH)]
        half_b = 0.5 * cb_ref[:, cols]

        def shifted(t, s):
            if t >= s:
                return x[t - s]
            i = SUBLANES - s + t
            prev = jnp.broadcast_to(tail_ref[i:i + 1, cols], (SUBLANES, LANES))
            return jnp.where(sub >= 1, pltpu.roll(x[seg + t - s], 1, axis=0), prev)

        xc = []
        for t in range(seg):
            acc = half_b + half_w[CONV_WIDTH - 1] * x[t]
            for s in range(1, CONV_WIDTH):
                acc = acc + half_w[CONV_WIDTH - 1 - s] * shifted(t, s)
            xc.append(acc)
        xc = jnp.concatenate(xc, axis=0)
        xc_ref[:, cols] = xc
        xcb_ref[:, cols] = xc.astype(BF16)

    tail_ref[...] = xr_ref[tc - SUBLANES:tc, :]

    for j in range(n_tiles):
        cols = slice(j * LANES, (j + 1) * LANES)
        ks = starts[j] * LANES
        pre = jnp.dot(xcb_ref[:, ks:ks + GATE_WINDOW], wband_ref[j], preferred_element_type=F32)
        tanh_r = jnp.tanh(pre[:, :LANES] + 0.5 * ba_ref[:, cols])
        tanh_i = jnp.tanh(pre[:, LANES:] + 0.5 * bx_ref[:, cols])
        nlam = -lam_ref[:, cols]
        softplus = jnp.maximum(nlam, 0.0) + jnp.log1p(jnp.exp(-jnp.abs(nlam)))
        half_c = (-0.5 * LRU_C) * softplus
        log_a = half_c * tanh_r + half_c
        a = jnp.exp(log_a)
        y = -jnp.tanh(log_a) * (a * a + 1.0)
        gated_x = (tanh_i + 1.0) * xc_ref[:, cols]
        u = y * lax.rsqrt(jnp.maximum(y, F32_TINY)) * gated_x
        a = a.reshape(seg, SUBLANES, LANES)
        u = u.reshape(seg, SUBLANES, LANES)

        h = u[0]
        decay = a[0]
        for t in range(1, seg):
            h = a[t] * h + u[t]
            decay = decay * a[t]
        for d in (1, 2, 4):
            m = sub >= d
            h = h + decay * jnp.where(m, pltpu.roll(h, d, axis=0), 0.0)
            decay = decay * jnp.where(m, pltpu.roll(decay, d, axis=0), 1.0)
        h_in = hcar_ref[:, cols]
        h_end = h + decay * h_in
        hcar_ref[:, cols] = jnp.broadcast_to(h_end[SUBLANES - 1:SUBLANES, :], (SUBLANES, LANES))
        h = jnp.where(sub >= 1, pltpu.roll(h_end, 1, axis=0), h_in)
        for t in range(seg):
            h = a[t] * h + u[t]
            hout_ref[j, pl.ds(t, SUBLANES, stride=pitch), :] = h
        for p in range(SUBLANES):
            y_ref[seg_rows(p), cols] = (gg_ref[seg_rows(p), cols] * hout_ref[j, slab_rows(p), :]).astype(y_ref.dtype)


def _rnn(xr_gg, conv_w, conv_b, wband, b_a, b_x, lam, tc=256):
    b, s, d_rnn = xr_gg.shape[0], xr_gg.shape[1], xr_gg.shape[2] // 2
    tc = _tile(s, tc)
    starts = _gate_window_starts(d_rnn)
    n_tiles = d_rnn // LANES
    slab = (n_tiles, tc + SUBLANES * SUBLANES, LANES)
    row = lambda v: v.reshape(1, d_rnn)
    full2 = lambda shape: pl.BlockSpec(shape, lambda bi, ci: (0, 0))
    chunk = pl.BlockSpec((None, tc, d_rnn), lambda bi, ci: (bi, ci, 0))
    gate_chunk = pl.BlockSpec((None, tc, d_rnn), lambda bi, ci: (bi, ci, 1))
    return pl.pallas_call(
        functools.partial(_rnn_kernel, tc=tc, d_rnn=d_rnn, starts=starts),
        grid=(b, s // tc),
        in_specs=[chunk, gate_chunk,
                  full2((CONV_WIDTH, d_rnn)), full2((1, d_rnn)),
                  pl.BlockSpec(wband.shape, lambda bi, ci: (0, 0, 0)),
                  full2((1, d_rnn)), full2((1, d_rnn)), full2((1, d_rnn))],
        out_specs=chunk,
        out_shape=jax.ShapeDtypeStruct((b, s, d_rnn), BF16),
        scratch_shapes=[pltpu.VMEM((SUBLANES, d_rnn), F32),
                        pltpu.VMEM(slab, F32),
                        pltpu.VMEM(slab, F32),
                        pltpu.VMEM((tc, d_rnn), F32),
                        pltpu.VMEM((tc, d_rnn), BF16),
                        pltpu.VMEM((SUBLANES, d_rnn), F32)],
        compiler_params=_cparams("parallel", "arbitrary"),
        name="rnn",
    )(xr_gg, xr_gg, conv_w, row(conv_b), wband, row(b_a), row(b_x), row(lam))


PRUNE_LOG = -104.0
SOFTPLUS_LINEAR = 80.0


def _attn_kernel(q_ref, k_ref, v_ref, tri_ref, o_ref, *, tq, hp):
    qi = pl.program_id(2)
    row = lax.broadcasted_iota(jnp.int32, (tq, tq), 0)
    col = lax.broadcasted_iota(jnp.int32, (tq, tq), 1)
    causal = col < row

    def sweep(last_kb, n_blocks, diag, state):
        rows = pl.ds(pl.multiple_of((last_kb - (n_blocks - 1)) * tq, tq), n_blocks * tq)
        zs, sps = [], []
        for hd in range(hp):
            lanes = slice(hd * HEAD_DIM, (hd + 1) * HEAD_DIM)
            z = lax.dot_general(q_ref[:, lanes], k_ref[rows, lanes], (((1,), (1,)), ((), ())),
                                preferred_element_type=F32)
            for blk in range(n_blocks):
                zb = z[:, blk * tq:(blk + 1) * tq]
                sp = jnp.where(zb > SOFTPLUS_LINEAR, zb, jnp.log(1.0 + jnp.exp(zb)))
                if diag and blk == n_blocks - 1:
                    sp = jnp.where(causal, sp, 0.0)
                zs.append(zb)
                sps.append(sp)
        excl = jnp.dot(jnp.concatenate([sp.astype(BF16) for sp in sps], axis=0), tri_ref[...],
                       preferred_element_type=F32)
        out = []
        for hd in range(hp):
            lanes = slice(hd * HEAD_DIM, (hd + 1) * HEAD_DIM)
            run, acc = state[hd]
            ws = [None] * n_blocks
            for blk in reversed(range(n_blocks)):
                u = hd * n_blocks + blk
                w = jnp.exp((zs[u] - sps[u]) + excl[u * tq:(u + 1) * tq] + run)
                if diag and blk == n_blocks - 1:
                    w = jnp.where(causal, w, 0.0)
                ws[blk] = w.astype(BF16)
                run = run - jnp.sum(sps[u], axis=-1, keepdims=True)
            acc = acc + jnp.dot(jnp.concatenate(ws, axis=1), v_ref[rows, lanes], preferred_element_type=F32)
            out.append((run, acc))
        return tuple(out)

    def run_max(state):
        m = jnp.max(state[0][0])
        for run, _ in state[1:]:
            m = jnp.maximum(m, jnp.max(run))
        return m

    zero = tuple((jnp.zeros((tq, 1), F32), jnp.zeros((tq, HEAD_DIM), F32)) for _ in range(hp))

    def store(state):
        for hd, (_, acc) in enumerate(state):
            o_ref[:, hd * HEAD_DIM:(hd + 1) * HEAD_DIM] = acc.astype(o_ref.dtype)

    @pl.when(qi == 0)
    def _():
        store(sweep(qi, 1, True, zero))

    @pl.when(qi > 0)
    def _():
        state = sweep(qi, 2, True, zero)

        def cond(c):
            kb, rmax, _ = c
            return jnp.logical_and(kb >= 0, rmax > PRUNE_LOG)

        def body(c):
            kb, _, st = c
            st = sweep(kb, 1, False, st)
            return kb - 1, run_max(st), st

        _, _, state = lax.while_loop(cond, body, (qi - 2, run_max(state), state))
        store(state)


def _attention(qk, v, tq=256, hp=8):
    b, s, _ = v.shape
    tq = _tile(s, tq)
    hw = hp * HEAD_DIM
    n_hg = N_HEADS // hp
    tri = jnp.asarray(-np.tril(np.ones((tq, tq), np.float32), -1), BF16)
    return pl.pallas_call(
        functools.partial(_attn_kernel, tq=tq, hp=hp),
        grid=(b, n_hg, s // tq),
        in_specs=[pl.BlockSpec((None, tq, hw), lambda bi, h, qi: (bi, qi, h)),
                  pl.BlockSpec((None, s, hw), lambda bi, h, qi: (bi, 0, n_hg + h)),
                  pl.BlockSpec((None, s, hw), lambda bi, h, qi: (bi, 0, h)),
                  pl.BlockSpec((tq, tq), lambda bi, h, qi: (0, 0))],
        out_specs=pl.BlockSpec((None, tq, hw), lambda bi, h, qi: (bi, qi, h)),
        out_shape=jax.ShapeDtypeStruct(v.shape, BF16),
        compiler_params=_cparams("parallel", "parallel", "arbitrary"),
        name="attention",
    )(qk, qk, v, tri)


def _mix_kernel(yr_ref, oa_ref, wr_ref, wa_ref, sr_ref, sa_ref, o_ref):
    ya = jnp.dot(yr_ref[...], wr_ref[...], preferred_element_type=F32)
    yb = jnp.dot(oa_ref[...], wa_ref[...], preferred_element_type=F32)
    o_ref[...] = (sr_ref[...] * ya + sa_ref[...] * yb).astype(o_ref.dtype)


def _mix(yr, oa, w_r, w_a, gates, tm=1024, tn=1024):
    t, d_rnn = yr.shape
    d_attn = oa.shape[1]
    n = w_r.shape[1]
    tm = _tile(t, tm)
    tn = _tile(n, tn)
    nj = n // tn
    return pl.pallas_call(
        _mix_kernel,
        grid=(nj, t // tm),
        in_specs=[pl.BlockSpec((tm, d_rnn), lambda j, i: (i, 0)),
                  pl.BlockSpec((tm, d_attn), lambda j, i: (i, 0)),
                  pl.BlockSpec((d_rnn, tn), lambda j, i: (0, j), pipeline_mode=pl.Buffered(1)),
                  pl.BlockSpec((d_attn, tn), lambda j, i: (0, j), pipeline_mode=pl.Buffered(1)),
                  pl.BlockSpec((tm, tn), lambda j, i: (i, j)),
                  pl.BlockSpec((tm, tn), lambda j, i: (i, nj + j))],
        out_specs=pl.BlockSpec((tm, tn), lambda j, i: (i, j)),
        out_shape=jax.ShapeDtypeStruct((t, n), BF16),
        compiler_params=_cparams("parallel", "parallel"),
        name="mix",
    )(yr, oa, w_r, w_a, gates, gates)


def _oproj_kernel(m_ref, w_ref, x_ref, g_ref, x1_ref, h_ref):
    x1 = x_ref[...] + jnp.dot(m_ref[...], w_ref[...], preferred_element_type=F32)
    x1_ref[...] = x1
    ms = jnp.mean(x1 * x1, axis=-1, keepdims=True)
    h_ref[...] = (x1 * lax.rsqrt(ms + EPS) * g_ref[...]).astype(h_ref.dtype)


def _oproj(mix, w_o, x, g_next, tm=512):
    t, d = x.shape
    tm = _tile(t, tm)
    return pl.pallas_call(
        _oproj_kernel,
        grid=(t // tm,),
        in_specs=[pl.BlockSpec((tm, d), lambda i: (i, 0)),
                  pl.BlockSpec((d, d), lambda i: (0, 0)),
                  pl.BlockSpec((tm, d), lambda i: (i, 0)),
                  pl.BlockSpec((1, d), lambda i: (0, 0))],
        out_specs=[pl.BlockSpec((tm, d), lambda i: (i, 0)),
                   pl.BlockSpec((tm, d), lambda i: (i, 0))],
        out_shape=[jax.ShapeDtypeStruct((t, d), F32), jax.ShapeDtypeStruct((t, d), BF16)],
        compiler_params=_cparams("parallel"),
        name="oproj",
    )(mix, w_o, x, g_next.reshape(1, d))


def _ffn_kernel(h_ref, wg_ref, wu_ref, wd_ref, x_ref, g_ref, x2_ref, h3_ref):
    f = pl.program_id(1)

    @pl.when(f == 0)
    def _():
        x2_ref[...] = x_ref[...]

    h = h_ref[...]
    g = jnp.dot(h, wg_ref[...], preferred_element_type=F32)
    u = jnp.dot(h, wu_ref[...], preferred_element_type=F32)
    act = (jax.nn.silu(g) * u).astype(BF16)
    x2_ref[...] += jnp.dot(act, wd_ref[...], preferred_element_type=F32)

    @pl.when(f == pl.num_programs(1) - 1)
    def _():
        x2 = x2_ref[...]
        ms = jnp.mean(x2 * x2, axis=-1, keepdims=True)
        h3_ref[...] = (x2 * lax.rsqrt(ms + EPS) * g_ref[...]).astype(h3_ref.dtype)


def _ffn(h2, w_gu, w_d, x1, g_next, tm=1024, tf=512):
    t, d = x1.shape
    d_ff = w_d.shape[0]
    tm = _tile(t, tm)
    tf = _tile(d_ff, tf)
    nf = d_ff // tf
    once = dict(pipeline_mode=pl.Buffered(1))
    return pl.pallas_call(
        _ffn_kernel,
        grid=(t // tm, d_ff // tf),
        in_specs=[pl.BlockSpec((tm, d), lambda i, f: (i, 0)),
                  pl.BlockSpec((d, tf), lambda i, f: (0, f)),
                  pl.BlockSpec((d, tf), lambda i, f: (0, nf + f)),
                  pl.BlockSpec((tf, d), lambda i, f: (f, 0)),
                  pl.BlockSpec((tm, d), lambda i, f: (i, 0)),
                  pl.BlockSpec((1, d), lambda i, f: (0, 0))],
        out_specs=[pl.BlockSpec((tm, d), lambda i, f: (i, 0), **once),
                   pl.BlockSpec((tm, d), lambda i, f: (i, 0), **once)],
        out_shape=[jax.ShapeDtypeStruct((t, d), F32), jax.ShapeDtypeStruct((t, d), BF16)],
        compiler_params=pltpu.CompilerParams(dimension_semantics=("parallel", "arbitrary"),
                                             vmem_limit_bytes=BIG_VMEM_LIMIT),
        name="ffn",
    )(h2, w_gu, w_gu, w_d, x1, g_next.reshape(1, d))


def _ple_kernel(h_ref, wg_ref, p_ref, wp_ref, x_ref, o_ref):
    rows = h_ref.shape[0] // PROJ_ROW_SLABS
    for m in range(PROJ_ROW_SLABS):
        sl = slice(m * rows, (m + 1) * rows)
        gate = jax.nn.sigmoid(jnp.dot(h_ref[sl, :], wg_ref[...], preferred_element_type=F32))
        emb = jnp.dot(p_ref[sl, :].astype(BF16), wp_ref[...].astype(BF16), preferred_element_type=F32)
        o_ref[sl, :] = x_ref[sl, :] + gate * emb


def _ple(h3, w_gate, p, w_proj, x2, tm=1024, tn=1024):
    t, d = x2.shape
    dp = p.shape[1]
    tm = _tile(t, tm)
    tn = _tile(d, tn)
    return pl.pallas_call(
        _ple_kernel,
        grid=(t // tm, d // tn),
        in_specs=[pl.BlockSpec((tm, d), lambda i, j: (i, 0)),
                  pl.BlockSpec((d, tn), lambda i, j: (0, j)),
                  pl.BlockSpec((tm, dp), lambda i, j: (i, 0)),
                  pl.BlockSpec((dp, tn), lambda i, j: (0, j)),
                  pl.BlockSpec((tm, tn), lambda i, j: (i, j))],
        out_specs=pl.BlockSpec((tm, tn), lambda i, j: (i, j)),
        out_shape=jax.ShapeDtypeStruct((t, d), F32),
        compiler_params=_cparams("parallel", "parallel"),
        name="ple",
    )(h3, w_gate, p, w_proj, x2)


def _layer(x, p, g_mix, w_in, conv_w, conv_b, w_rg_a, b_rg_a, w_rg_x, b_rg_x, lru_lambda, q_gain,
           k_gain, w_rnn_out, w_attn_out, w_o, g_ffn, w_ffn_gu, w_ffn_down, g_ple, w_ple_gate,
           w_ple_proj):
    b, s, d = x.shape
    t = b * s
    d_rnn = w_rnn_out.shape[0]
    d_attn = w_attn_out.shape[0]
    x2d = x.reshape(t, d)

    o0 = 0
    o1 = o0 + d_rnn
    o2 = o1 + d_rnn
    o3 = o2 + 2 * d_attn
    o4 = o3 + d_attn
    o5 = o4 + 2 * d
    assert o5 == w_in.shape[1]

    h = _rmsnorm(x2d, g_mix)
    xr_gg = _proj_rnn_inputs(h, w_in, o0, d_rnn)
    qk_gain = jnp.concatenate([jnp.tile(q_gain * (1.0 / math.sqrt(HEAD_DIM)), N_HEADS),
                               jnp.tile(k_gain, N_HEADS)]).reshape(1, 2 * d_attn)
    qk, w_gu_b, w_attn_out_b = _proj(h, w_in, o2, 2 * d_attn, "headnorm", BF16, gain=qk_gain,
                                     casts=(w_ffn_gu, w_attn_out))
    v, w_rnn_out_b = _proj(h, w_in, o3, d_attn, "none", BF16, casts=(w_rnn_out,))
    gates, w_down_b, w_o_b, w_ple_gate_b = _proj(h, w_in, o4, 2 * d, "sigmoid", BF16,
                                                 casts=(w_ffn_down, w_o, w_ple_gate))

    wband = _band_gate_weights(w_rg_a, w_rg_x)
    yr = _rnn(xr_gg.reshape(b, s, 2 * d_rnn), conv_w, conv_b, wband, b_rg_a, b_rg_x, lru_lambda)
    oa = _attention(qk.reshape(b, s, 2 * d_attn), v.reshape(b, s, d_attn))

    mix = _mix(yr.reshape(t, d_rnn), oa.reshape(t, d_attn), w_rnn_out_b, w_attn_out_b, gates)
    x1, h2 = _oproj(mix, w_o_b, x2d, g_ffn)
    x2, h3 = _ffn(h2, w_gu_b, w_down_b, x1, g_ple)
    out = _ple(h3, w_ple_gate_b, p.reshape(t, -1), w_ple_proj, x2)
    return out.reshape(b, s, d)


def kernel(x, p, g_mix, w_in, conv_w, conv_b, w_rg_a, b_rg_a, w_rg_x, b_rg_x, lru_lambda, q_gain,
           k_gain, w_rnn_out, w_attn_out, w_o, g_ffn, w_ffn_gu, w_ffn_down, g_ple, w_ple_gate,
           w_ple_proj):
    params = (g_mix, w_in, conv_w, conv_b, w_rg_a, b_rg_a, w_rg_x, b_rg_x, lru_lambda, q_gain,
              k_gain, w_rnn_out, w_attn_out, w_o, g_ffn, w_ffn_gu, w_ffn_down, g_ple, w_ple_gate,
              w_ple_proj)
    for i in range(p.shape[0]):
        x = _layer(x, p[i], *[a[i] for a in params])
    return x
```

```python
import functools
import math

import jax
import jax.numpy as jnp
import numpy as np
from jax import lax
from jax.experimental import pallas as pl
from jax.experimental.pallas import tpu as pltpu

F32 = jnp.float32
BF16 = jnp.bfloat16

EPS = 1e-6
LRU_C = 8.0
N_HEADS = 16
HEAD_DIM = 128
N_RNN_BLOCKS = 16
CONV_WIDTH = 4

LANES = 128
SUBLANES = 8
GATE_WINDOW = 4 * LANES
VMEM_LIMIT = 56 * 1024 * 1024
F32_TINY = float(np.finfo(np.float32).tiny)


def _cparams(*sem):
    return pltpu.CompilerParams(dimension_semantics=sem, vmem_limit_bytes=VMEM_LIMIT)


def _tile(n, pref):
    t = min(n, pref)
    assert n % t == 0, (n, t)
    return t


def _rmsnorm_kernel(x_ref, g_ref, o_ref):
    x = x_ref[...]
    ms = jnp.mean(x * x, axis=-1, keepdims=True)
    o_ref[...] = (x * lax.rsqrt(ms + EPS) * g_ref[...]).astype(o_ref.dtype)


def _rmsnorm(x, g, tm=512):
    t, d = x.shape
    tm = _tile(t, tm)
    return pl.pallas_call(
        _rmsnorm_kernel,
        grid=(t // tm,),
        in_specs=[pl.BlockSpec((tm, d), lambda i: (i, 0)),
                  pl.BlockSpec((1, d), lambda i: (0, 0))],
        out_specs=pl.BlockSpec((tm, d), lambda i: (i, 0)),
        out_shape=jax.ShapeDtypeStruct((t, d), BF16),
        compiler_params=_cparams("parallel"),
        name="rmsnorm",
    )(x, g.reshape(1, d))


def _head_norm(acc, gain):
    outs = []
    for c in range(acc.shape[1] // HEAD_DIM):
        blk = acc[:, c * HEAD_DIM:(c + 1) * HEAD_DIM]
        ms = jnp.mean(blk * blk, axis=-1, keepdims=True)
        outs.append(blk * lax.rsqrt(ms + EPS) * gain[:, c * HEAD_DIM:(c + 1) * HEAD_DIM])
    return jnp.concatenate(outs, axis=1)


PROJ_ROW_SLABS = 4


def _proj_kernel(h_ref, w_ref, *rest, kind, n_cast):
    wb_ref = rest[-1]
    cast_out = rest[len(rest) - 1 - n_cast:-1]
    o_ref = rest[-2 - n_cast]
    cast_in = rest[-2 - 2 * n_cast:-2 - n_cast]
    extra = rest[:-2 - 2 * n_cast]

    @pl.when(pl.program_id(1) == 0)
    def _():
        wb_ref[...] = w_ref[...].astype(BF16)

    rows = h_ref.shape[0] // PROJ_ROW_SLABS
    for m in range(PROJ_ROW_SLABS):
        sl = slice(m * rows, (m + 1) * rows)
        acc = jnp.dot(h_ref[sl, :], wb_ref[...], preferred_element_type=F32)
        if kind == "sigmoid":
            acc = jax.nn.sigmoid(acc)
        elif kind == "headnorm":
            acc = _head_norm(acc, extra[0][...])
        o_ref[sl, :] = acc.astype(o_ref.dtype)

    for src_ref, dst_ref in zip(cast_in, cast_out):
        dst_ref[...] = src_ref[...].astype(BF16)


def _proj(h, w, col0, n, kind, out_dtype, gain=None, casts=(), tm=1024, tn=1024):
    t, d = h.shape
    tm = _tile(t, tm)
    tn = _tile(n, tn)
    assert col0 % LANES == 0 and tn % LANES == 0 and tm % (PROJ_ROW_SLABS * 2 * SUBLANES) == 0
    n_i = t // tm
    n_steps = (n // tn) * n_i
    in_specs = [pl.BlockSpec((tm, d), lambda j, i: (i, 0)),
                pl.BlockSpec((pl.Element(d), pl.Element(tn)),
                             lambda j, i: (0, pl.multiple_of(col0 + j * tn, LANES)))]
    args = [h, w]
    if gain is not None:
        in_specs.append(pl.BlockSpec((1, tn), lambda j, i: (0, j)))
        args.append(gain)
    out_specs = [pl.BlockSpec((tm, tn), lambda j, i: (i, j))]
    out_shape = [jax.ShapeDtypeStruct((t, n), out_dtype)]
    for c in casts:
        rows = c.shape[0] // n_steps
        assert rows * n_steps == c.shape[0] and rows % (2 * SUBLANES) == 0, (c.shape, n_steps)
        spec = pl.BlockSpec((rows, c.shape[1]), lambda j, i: (j * n_i + i, 0))
        in_specs.append(spec)
        args.append(c)
        out_specs.append(spec)
        out_shape.append(jax.ShapeDtypeStruct(c.shape, BF16))
    return pl.pallas_call(
        functools.partial(_proj_kernel, kind=kind, n_cast=len(casts)),
        grid=(n // tn, n_i),
        in_specs=in_specs,
        out_specs=out_specs,
        out_shape=out_shape,
        scratch_shapes=[pltpu.VMEM((d, tn), BF16)],
        compiler_params=_cparams("parallel", "arbitrary"),
        name="proj_" + kind,
    )(*args)


BIG_VMEM_LIMIT = 60 * 1024 * 1024


def _proj_rnn_inputs_kernel(h_ref, w_ref, o_ref, wb_ref, *, row_slabs):
    @pl.when(pl.program_id(1) == 0)
    def _():
        wb_ref[...] = w_ref[...].astype(BF16)

    rows = h_ref.shape[0] // row_slabs

    def body(gelu):
        for m in range(row_slabs):
            sl = slice(m * rows, (m + 1) * rows)
            acc = jnp.dot(h_ref[sl, :], wb_ref[...], preferred_element_type=F32)
            o_ref[sl, :] = jax.nn.gelu(acc) if gelu else acc

    @pl.when(pl.program_id(0) == 0)
    def _():
        body(False)

    @pl.when(pl.program_id(0) == 1)
    def _():
        body(True)


def _proj_rnn_inputs(h, w, col0, d_rnn, tm=512, row_slabs=2):
    t, d = h.shape
    tm = _tile(t, tm)
    assert col0 % LANES == 0 and d_rnn % LANES == 0 and tm % (row_slabs * 2 * SUBLANES) == 0
    return pl.pallas_call(
        functools.partial(_proj_rnn_inputs_kernel, row_slabs=row_slabs),
        grid=(2, t // tm),
        in_specs=[pl.BlockSpec((tm, d), lambda j, i: (i, 0)),
                  pl.BlockSpec((pl.Element(d), pl.Element(d_rnn)),
                               lambda j, i: (0, pl.multiple_of(col0 + j * d_rnn, LANES)),
                               pipeline_mode=pl.Buffered(1))],
        out_specs=pl.BlockSpec((tm, d_rnn), lambda j, i: (i, j)),
        out_shape=jax.ShapeDtypeStruct((t, 2 * d_rnn), F32),
        scratch_shapes=[pltpu.VMEM((d, d_rnn), BF16)],
        compiler_params=pltpu.CompilerParams(dimension_semantics=("arbitrary", "arbitrary"),
                                             vmem_limit_bytes=BIG_VMEM_LIMIT),
        name="proj_rnn_inputs",
    )(h, w)


def _gate_window_starts(d_rnn):
    blk = d_rnn // N_RNN_BLOCKS
    n_tiles = d_rnn // LANES
    starts = []
    for j in range(n_tiles):
        b0 = (LANES * j) // blk
        b1 = (LANES * j + LANES - 1) // blk
        s = min((blk * b0) // LANES, n_tiles - GATE_WINDOW // LANES)
        assert s * LANES <= blk * b0 and blk * (b1 + 1) <= s * LANES + GATE_WINDOW
        starts.append(s)
    return starts


def _band_gate_weights(w_a, w_x):
    nb, blk, _ = w_a.shape
    d_rnn = nb * blk
    starts = _gate_window_starts(d_rnn)

    def band(w):
        tiles = []
        for j, s in enumerate(starts):
            c_lo, c_hi = LANES * j, LANES * (j + 1)
            pieces = []
            for b in range(c_lo // blk, (c_hi - 1) // blk + 1):
                oc0, oc1 = max(c_lo, blk * b), min(c_hi, blk * (b + 1))
                r0 = blk * b - s * LANES
                pieces.append(jnp.pad(w[b, :, oc0 - blk * b:oc1 - blk * b],
                                      ((r0, GATE_WINDOW - blk - r0), (0, 0))))
            tiles.append(jnp.concatenate(pieces, axis=1))
        return jnp.stack(tiles)

    return jnp.concatenate([band(w_a), band(w_x)], axis=-1).astype(BF16)


def _rnn_kernel(xr_ref, gg_ref, cw_ref, cb_ref, wband_ref, ba_ref, bx_ref, lam_ref, y_ref,
                tail_ref, xin_ref, hout_ref, xc_ref, xcb_ref, hcar_ref, *, tc, d_rnn, starts):
    n_tiles = d_rnn // LANES
    seg = tc // SUBLANES
    pitch = seg + SUBLANES

    @pl.when(pl.program_id(1) == 0)
    def _():
        tail_ref[...] = jnp.zeros_like(tail_ref)
        hcar_ref[...] = jnp.zeros_like(hcar_ref)

    sub = lax.broadcasted_iota(jnp.int32, (SUBLANES, LANES), 0)

    def seg_rows(p):
        return slice(p * seg, (p + 1) * seg)

    def slab_rows(p):
        return slice(p * pitch, p * pitch + seg)

    for j in range(n_tiles):
        cols = slice(j * LANES, (j + 1) * LANES)
        for p in range(SUBLANES):
            xin_ref[j, slab_rows(p), :] = xr_ref[seg_rows(p), cols]
        x = [xin_ref[j, pl.ds(t, SUBLANES, stride=pitch), :] for t in range(seg)]
        half_w = [0.5 * cw_ref[k:k + 1, cols] for k in range(CONV_WIDTH)]
        half_b = 0.5 * cb_ref[:, cols]

        def shifted(t, s):
            if t >= s:
                return x[t - s]
            i = SUBLANES - s + t
            prev = jnp.broadcast_to(tail_ref[i:i + 1, cols], (SUBLANES, LANES))
            return jnp.where(sub >= 1, pltpu.roll(x[seg + t - s], 1, axis=0), prev)

        xc = []
        for t in range(seg):
            acc = half_b + half_w[CONV_WIDTH - 1] * x[t]
            for s in range(1, CONV_WIDTH):
                acc = acc + half_w[CONV_WIDTH - 1 - s] * shifted(t, s)
            xc.append(acc)
        xc = jnp.concatenate(xc, axis=0)
        xc_ref[:, cols] = xc
        xcb_ref[:, cols] = xc.astype(BF16)

    tail_ref[...] = xr_ref[tc - SUBLANES:tc, :]

    for j in range(n_tiles):
        cols = slice(j * LANES, (j + 1) * LANES)
        ks = starts[j] * LANES
        pre = jnp.dot(xcb_ref[:, ks:ks + GATE_WINDOW], wband_ref[j], preferred_element_type=F32)
        tanh_r = jnp.tanh(pre[:, :LANES] + 0.5 * ba_ref[:, cols])
        tanh_i = jnp.tanh(pre[:, LANES:] + 0.5 * bx_ref[:, cols])
        nlam = -lam_ref[:, cols]
        softplus = jnp.maximum(nlam, 0.0) + jnp.log1p(jnp.exp(-jnp.abs(nlam)))
        half_c = (-0.5 * LRU_C) * softplus
        log_a = half_c * tanh_r + half_c
        a = jnp.exp(log_a)
        y = -jnp.tanh(log_a) * (a * a + 1.0)
        gated_x = (tanh_i + 1.0) * xc_ref[:, cols]
        u = y * lax.rsqrt(jnp.maximum(y, F32_TINY)) * gated_x
        a = a.reshape(seg, SUBLANES, LANES)
        u = u.reshape(seg, SUBLANES, LANES)

        h = u[0]
        decay = a[0]
        for t in range(1, seg):
            h = a[t] * h + u[t]
            decay = decay * a[t]
        for d in (1, 2, 4):
            m = sub >= d
            h = h + decay * jnp.where(m, pltpu.roll(h, d, axis=0), 0.0)
            decay = decay * jnp.where(m, pltpu.roll(decay, d, axis=0), 1.0)
        h_in = hcar_ref[:, cols]
        h_end = h + decay * h_in
        hcar_ref[:, cols] = jnp.broadcast_to(h_end[SUBLANES - 1:SUBLANES, :], (SUBLANES, LANES))
        h = jnp.where(sub >= 1, pltpu.roll(h_end, 1, axis=0), h_in)
        for t in range(seg):
            h = a[t] * h + u[t]
            hout_ref[j, pl.ds(t, SUBLANES, stride=pitch), :] = h
        for p in range(SUBLANES):
            y_ref[seg_rows(p), cols] = (gg_ref[seg_rows(p), cols] * hout_ref[j, slab_rows(p), :]).astype(y_ref.dtype)


def _rnn(xr_gg, conv_w, conv_b, wband, b_a, b_x, lam, tc=256):
    b, s, d_rnn = xr_gg.shape[0], xr_gg.shape[1], xr_gg.shape[2] // 2
    tc = _tile(s, tc)
    starts = _gate_window_starts(d_rnn)
    n_tiles = d_rnn // LANES
    slab = (n_tiles, tc + SUBLANES * SUBLANES, LANES)
    row = lambda v: v.reshape(1, d_rnn)
    full2 = lambda shape: pl.BlockSpec(shape, lambda bi, ci: (0, 0))
    chunk = pl.BlockSpec((None, tc, d_rnn), lambda bi, ci: (bi, ci, 0))
    gate_chunk = pl.BlockSpec((None, tc, d_rnn), lambda bi, ci: (bi, ci, 1))
    return pl.pallas_call(
        functools.partial(_rnn_kernel, tc=tc, d_rnn=d_rnn, starts=starts),
        grid=(b, s // tc),
        in_specs=[chunk, gate_chunk,
                  full2((CONV_WIDTH, d_rnn)), full2((1, d_rnn)),
                  pl.BlockSpec(wband.shape, lambda bi, ci: (0, 0, 0)),
                  full2((1, d_rnn)), full2((1, d_rnn)), full2((1, d_rnn))],
        out_specs=chunk,
        out_shape=jax.ShapeDtypeStruct((b, s, d_rnn), BF16),
        scratch_shapes=[pltpu.VMEM((SUBLANES, d_rnn), F32),
                        pltpu.VMEM(slab, F32),
                        pltpu.VMEM(slab, F32),
                        pltpu.VMEM((tc, d_rnn), F32),
                        pltpu.VMEM((tc, d_rnn), BF16),
                        pltpu.VMEM((SUBLANES, d_rnn), F32)],
        compiler_params=_cparams("parallel", "arbitrary"),
        name="rnn",
    )(xr_gg, xr_gg, conv_w, row(conv_b), wband, row(b_a), row(b_x), row(lam))


PRUNE_LOG = -104.0
SOFTPLUS_LINEAR = 80.0


def _attn_kernel(q_ref, k_ref, v_ref, tri_ref, o_ref, *, tq, hp):
    qi = pl.program_id(2)
    row = lax.broadcasted_iota(jnp.int32, (tq, tq), 0)
    col = lax.broadcasted_iota(jnp.int32, (tq, tq), 1)
    causal = col < row

    def sweep(last_kb, n_blocks, diag, state):
        rows = pl.ds(pl.multiple_of((last_kb - (n_blocks - 1)) * tq, tq), n_blocks * tq)
        zs, sps = [], []
        for hd in range(hp):
            lanes = slice(hd * HEAD_DIM, (hd + 1) * HEAD_DIM)
            z = lax.dot_general(q_ref[:, lanes], k_ref[rows, lanes], (((1,), (1,)), ((), ())),
                                preferred_element_type=F32)
            for blk in range(n_blocks):
                zb = z[:, blk * tq:(blk + 1) * tq]
                sp = jnp.where(zb > SOFTPLUS_LINEAR, zb, jnp.log(1.0 + jnp.exp(zb)))
                if diag and blk == n_blocks - 1:
                    sp = jnp.where(causal, sp, 0.0)
                zs.append(zb)
                sps.append(sp)
        excl = jnp.dot(jnp.concatenate([sp.astype(BF16) for sp in sps], axis=0), tri_ref[...],
                       preferred_element_type=F32)
        out = []
        for hd in range(hp):
            lanes = slice(hd * HEAD_DIM, (hd + 1) * HEAD_DIM)
            run, acc = state[hd]
            ws = [None] * n_blocks
            for blk in reversed(range(n_blocks)):
                u = hd * n_blocks + blk
                w = jnp.exp((zs[u] - sps[u]) + excl[u * tq:(u + 1) * tq] + run)
                if diag and blk == n_blocks - 1:
                    w = jnp.where(causal, w, 0.0)
                ws[blk] = w.astype(BF16)
                run = run - jnp.sum(sps[u], axis=-1, keepdims=True)
            acc = acc + jnp.dot(jnp.concatenate(ws, axis=1), v_ref[rows, lanes], preferred_element_type=F32)
            out.append((run, acc))
        return tuple(out)

    def run_max(state):
        m = jnp.max(state[0][0])
        for run, _ in state[1:]:
            m = jnp.maximum(m, jnp.max(run))
        return m

    zero = tuple((jnp.zeros((tq, 1), F32), jnp.zeros((tq, HEAD_DIM), F32)) for _ in range(hp))

    def store(state):
        for hd, (_, acc) in enumerate(state):
            o_ref[:, hd * HEAD_DIM:(hd + 1) * HEAD_DIM] = acc.astype(o_ref.dtype)

    @pl.when(qi == 0)
    def _():
        store(sweep(qi, 1, True, zero))

    @pl.when(qi > 0)
    def _():
        state = sweep(qi, 2, True, zero)

        def cond(c):
            kb, rmax, _ = c
            return jnp.logical_and(kb >= 0, rmax > PRUNE_LOG)

        def body(c):
            kb, _, st = c
            st = sweep(kb, 1, False, st)
            return kb - 1, run_max(st), st

        _, _, state = lax.while_loop(cond, body, (qi - 2, run_max(state), state))
        store(state)


def _attention(qk, v, tq=256, hp=8):
    b, s, _ = v.shape
    tq = _tile(s, tq)
    hw = hp * HEAD_DIM
    n_hg = N_HEADS // hp
    tri = jnp.asarray(-np.tril(np.ones((tq, tq), np.float32), -1), BF16)
    return pl.pallas_call(
        functools.partial(_attn_kernel, tq=tq, hp=hp),
        grid=(b, n_hg, s // tq),
        in_specs=[pl.BlockSpec((None, tq, hw), lambda bi, h, qi: (bi, qi, h)),
                  pl.BlockSpec((None, s, hw), lambda bi, h, qi: (bi, 0, n_hg + h)),
                  pl.BlockSpec((None, s, hw), lambda bi, h, qi: (bi, 0, h)),
                  pl.BlockSpec((tq, tq), lambda bi, h, qi: (0, 0))],
        out_specs=pl.BlockSpec((None, tq, hw), lambda bi, h, qi: (bi, qi, h)),
        out_shape=jax.ShapeDtypeStruct(v.shape, BF16),
        compiler_params=_cparams("parallel", "parallel", "arbitrary"),
        name="attention",
    )(qk, qk, v, tri)


def _mix_kernel(yr_ref, oa_ref, wr_ref, wa_ref, sr_ref, sa_ref, o_ref, *, row_slabs):
    rows = yr_ref.shape[0] // row_slabs
    for m in range(row_slabs):
        sl = slice(m * rows, (m + 1) * rows)
        ya = jnp.dot(yr_ref[sl, :], wr_ref[...], preferred_element_type=F32)
        yb = jnp.dot(oa_ref[sl, :], wa_ref[...], preferred_element_type=F32)
        o_ref[sl, :] = (sr_ref[sl, :] * ya + sa_ref[sl, :] * yb).astype(o_ref.dtype)


def _mix(yr, oa, w_r, w_a, gates, tm=512, row_slabs=2):
    t, d_rnn = yr.shape
    d_attn = oa.shape[1]
    n = w_r.shape[1]
    tm = _tile(t, tm)
    once = dict(pipeline_mode=pl.Buffered(1))
    return pl.pallas_call(
        functools.partial(_mix_kernel, row_slabs=row_slabs),
        grid=(t // tm,),
        in_specs=[pl.BlockSpec((tm, d_rnn), lambda i: (i, 0)),
                  pl.BlockSpec((tm, d_attn), lambda i: (i, 0)),
                  pl.BlockSpec((d_rnn, n), lambda i: (0, 0), **once),
                  pl.BlockSpec((d_attn, n), lambda i: (0, 0), **once),
                  pl.BlockSpec((tm, n), lambda i: (i, 0)),
                  pl.BlockSpec((tm, n), lambda i: (i, 1))],
        out_specs=pl.BlockSpec((tm, n), lambda i: (i, 0)),
        out_shape=jax.ShapeDtypeStruct((t, n), BF16),
        compiler_params=_cparams("parallel"),
        name="mix",
    )(yr, oa, w_r, w_a, gates, gates)


def _oproj_kernel(m_ref, w_ref, x_ref, g_ref, x1_ref, h_ref):
    x1 = x_ref[...] + jnp.dot(m_ref[...], w_ref[...], preferred_element_type=F32)
    x1_ref[...] = x1
    ms = jnp.mean(x1 * x1, axis=-1, keepdims=True)
    h_ref[...] = (x1 * lax.rsqrt(ms + EPS) * g_ref[...]).astype(h_ref.dtype)


def _oproj(mix, w_o, x, g_next, tm=512):
    t, d = x.shape
    tm = _tile(t, tm)
    return pl.pallas_call(
        _oproj_kernel,
        grid=(t // tm,),
        in_specs=[pl.BlockSpec((tm, d), lambda i: (i, 0)),
                  pl.BlockSpec((d, d), lambda i: (0, 0)),
                  pl.BlockSpec((tm, d), lambda i: (i, 0)),
                  pl.BlockSpec((1, d), lambda i: (0, 0))],
        out_specs=[pl.BlockSpec((tm, d), lambda i: (i, 0)),
                   pl.BlockSpec((tm, d), lambda i: (i, 0))],
        out_shape=[jax.ShapeDtypeStruct((t, d), F32), jax.ShapeDtypeStruct((t, d), BF16)],
        compiler_params=_cparams("parallel"),
        name="oproj",
    )(mix, w_o, x, g_next.reshape(1, d))


def _ffn_kernel(h_ref, wg_ref, wu_ref, wd_ref, x_ref, g_ref, x2_ref, h3_ref):
    f = pl.program_id(1)

    @pl.when(f == 0)
    def _():
        x2_ref[...] = x_ref[...]

    h = h_ref[...]
    g = jnp.dot(h, wg_ref[...], preferred_element_type=F32)
    u = jnp.dot(h, wu_ref[...], preferred_element_type=F32)
    act = (jax.nn.silu(g) * u).astype(BF16)
    x2_ref[...] += jnp.dot(act, wd_ref[...], preferred_element_type=F32)

    @pl.when(f == pl.num_programs(1) - 1)
    def _():
        x2 = x2_ref[...]
        ms = jnp.mean(x2 * x2, axis=-1, keepdims=True)
        h3_ref[...] = (x2 * lax.rsqrt(ms + EPS) * g_ref[...]).astype(h3_ref.dtype)


def _ffn(h2, w_gu, w_d, x1, g_next, tm=1024, tf=512):
    t, d = x1.shape
    d_ff = w_d.shape[0]
    tm = _tile(t, tm)
    tf = _tile(d_ff, tf)
    nf = d_ff // tf
    once = dict(pipeline_mode=pl.Buffered(1))
    return pl.pallas_call(
        _ffn_kernel,
        grid=(t // tm, d_ff // tf),
        in_specs=[pl.BlockSpec((tm, d), lambda i, f: (i, 0)),
                  pl.BlockSpec((d, tf), lambda i, f: (0, f)),
                  pl.BlockSpec((d, tf), lambda i, f: (0, nf + f)),
                  pl.BlockSpec((tf, d), lambda i, f: (f, 0)),
                  pl.BlockSpec((tm, d), lambda i, f: (i, 0)),
                  pl.BlockSpec((1, d), lambda i, f: (0, 0))],
        out_specs=[pl.BlockSpec((tm, d), lambda i, f: (i, 0), **once),
                   pl.BlockSpec((tm, d), lambda i, f: (i, 0), **once)],
        out_shape=[jax.ShapeDtypeStruct((t, d), F32), jax.ShapeDtypeStruct((t, d), BF16)],
        compiler_params=pltpu.CompilerParams(dimension_semantics=("parallel", "arbitrary"),
                                             vmem_limit_bytes=BIG_VMEM_LIMIT),
        name="ffn",
    )(h2, w_gu, w_gu, w_d, x1, g_next.reshape(1, d))


def _ple_kernel(h_ref, wg_ref, p_ref, wp_ref, x_ref, o_ref, *, row_slabs):
    rows = h_ref.shape[0] // row_slabs
    for m in range(row_slabs):
        sl = slice(m * rows, (m + 1) * rows)
        gate = jax.nn.sigmoid(jnp.dot(h_ref[sl, :], wg_ref[...], preferred_element_type=F32))
        emb = jnp.dot(p_ref[sl, :].astype(BF16), wp_ref[...].astype(BF16), preferred_element_type=F32)
        o_ref[sl, :] = x_ref[sl, :] + gate * emb


def _ple(h3, w_gate, p, w_proj, x2, tm=512, row_slabs=2):
    t, d = x2.shape
    dp = p.shape[1]
    tm = _tile(t, tm)
    once = dict(pipeline_mode=pl.Buffered(1))
    return pl.pallas_call(
        functools.partial(_ple_kernel, row_slabs=row_slabs),
        grid=(t // tm,),
        in_specs=[pl.BlockSpec((tm, d), lambda i: (i, 0)),
                  pl.BlockSpec((d, d), lambda i: (0, 0), **once),
                  pl.BlockSpec((tm, dp), lambda i: (i, 0)),
                  pl.BlockSpec((dp, d), lambda i: (0, 0), **once),
                  pl.BlockSpec((tm, d), lambda i: (i, 0))],
        out_specs=pl.BlockSpec((tm, d), lambda i: (i, 0)),
        out_shape=jax.ShapeDtypeStruct((t, d), F32),
        compiler_params=_cparams("parallel"),
        name="ple",
    )(h3, w_gate, p, w_proj, x2)


def _layer(x, p, g_mix, w_in, conv_w, conv_b, w_rg_a, b_rg_a, w_rg_x, b_rg_x, lru_lambda, q_gain,
           k_gain, w_rnn_out, w_attn_out, w_o, g_ffn, w_ffn_gu, w_ffn_down, g_ple, w_ple_gate,
           w_ple_proj):
    b, s, d = x.shape
    t = b * s
    d_rnn = w_rnn_out.shape[0]
    d_attn = w_attn_out.shape[0]
    x2d = x.reshape(t, d)

    o0 = 0
    o1 = o0 + d_rnn
    o2 = o1 + d_rnn
    o3 = o2 + 2 * d_attn
    o4 = o3 + d_attn
    o5 = o4 + 2 * d
    assert o5 == w_in.shape[1]

    h = _rmsnorm(x2d, g_mix)
    xr_gg = _proj_rnn_inputs(h, w_in, o0, d_rnn)
    qk_gain = jnp.concatenate([jnp.tile(q_gain * (1.0 / math.sqrt(HEAD_DIM)), N_HEADS),
                               jnp.tile(k_gain, N_HEADS)]).reshape(1, 2 * d_attn)
    qk, w_gu_b, w_attn_out_b = _proj(h, w_in, o2, 2 * d_attn, "headnorm", BF16, gain=qk_gain,
                                     casts=(w_ffn_gu, w_attn_out))
    v, w_rnn_out_b = _proj(h, w_in, o3, d_attn, "none", BF16, casts=(w_rnn_out,))
    gates, w_down_b, w_o_b, w_ple_gate_b = _proj(h, w_in, o4, 2 * d, "sigmoid", BF16,
                                                 casts=(w_ffn_down, w_o, w_ple_gate))

    wband = _band_gate_weights(w_rg_a, w_rg_x)
    yr = _rnn(xr_gg.reshape(b, s, 2 * d_rnn), conv_w, conv_b, wband, b_rg_a, b_rg_x, lru_lambda)
    oa = _attention(qk.reshape(b, s, 2 * d_attn), v.reshape(b, s, d_attn))

    mix = _mix(yr.reshape(t, d_rnn), oa.reshape(t, d_attn), w_rnn_out_b, w_attn_out_b, gates)
    x1, h2 = _oproj(mix, w_o_b, x2d, g_ffn)
    x2, h3 = _ffn(h2, w_gu_b, w_down_b, x1, g_ple)
    out = _ple(h3, w_ple_gate_b, p.reshape(t, -1), w_ple_proj, x2)
    return out.reshape(b, s, d)


def kernel(x, p, g_mix, w_in, conv_w, conv_b, w_rg_a, b_rg_a, w_rg_x, b_rg_x, lru_lambda, q_gain,
           k_gain, w_rnn_out, w_attn_out, w_o, g_ffn, w_ffn_gu, w_ffn_down, g_ple, w_ple_gate,
           w_ple_proj):
    params = (g_mix, w_in, conv_w, conv_b, w_rg_a, b_rg_a, w_rg_x, b_rg_x, lru_lambda, q_gain,
              k_gain, w_rnn_out, w_attn_out, w_o, g_ffn, w_ffn_gu, w_ffn_down, g_ple, w_ple_gate,
              w_ple_proj)
    for i in range(p.shape[0]):
        x = _layer(x, p[i], *[a[i] for a in params])
    return x
```

```python
import functools
import math

import jax
import jax.numpy as jnp
import numpy as np
from jax import lax
from jax.experimental import pallas as pl
from jax.experimental.pallas import tpu as pltpu

F32 = jnp.float32
BF16 = jnp.bfloat16

EPS = 1e-6
LRU_C = 8.0
N_HEADS = 16
HEAD_DIM = 128
N_RNN_BLOCKS = 16
CONV_WIDTH = 4

LANES = 128
SUBLANES = 8
GATE_WINDOW = 4 * LANES
VMEM_LIMIT = 56 * 1024 * 1024
F32_TINY = float(np.finfo(np.float32).tiny)


def _cparams(*sem):
    return pltpu.CompilerParams(dimension_semantics=sem, vmem_limit_bytes=VMEM_LIMIT)


def _tile(n, pref):
    t = min(n, pref)
    assert n % t == 0, (n, t)
    return t


def _rmsnorm_kernel(x_ref, g_ref, o_ref):
    x = x_ref[...]
    ms = jnp.mean(x * x, axis=-1, keepdims=True)
    o_ref[...] = (x * lax.rsqrt(ms + EPS) * g_ref[...]).astype(o_ref.dtype)


def _rmsnorm(x, g, tm=1024):
    t, d = x.shape
    tm = _tile(t, tm)
    return pl.pallas_call(
        _rmsnorm_kernel,
        grid=(t // tm,),
        in_specs=[pl.BlockSpec((tm, d), lambda i: (i, 0)),
                  pl.BlockSpec((1, d), lambda i: (0, 0))],
        out_specs=pl.BlockSpec((tm, d), lambda i: (i, 0)),
        out_shape=jax.ShapeDtypeStruct((t, d), BF16),
        compiler_params=_cparams("parallel"),
        name="rmsnorm",
    )(x, g.reshape(1, d))


def _head_norm(acc, gain):
    outs = []
    for c in range(acc.shape[1] // HEAD_DIM):
        blk = acc[:, c * HEAD_DIM:(c + 1) * HEAD_DIM]
        ms = jnp.mean(blk * blk, axis=-1, keepdims=True)
        outs.append(blk * lax.rsqrt(ms + EPS) * gain[:, c * HEAD_DIM:(c + 1) * HEAD_DIM])
    return jnp.concatenate(outs, axis=1)


PROJ_ROW_SLABS = 4


def _proj_kernel(h_ref, w_ref, *rest, kind, n_cast):
    wb_ref = rest[-1]
    cast_out = rest[len(rest) - 1 - n_cast:-1]
    o_ref = rest[-2 - n_cast]
    cast_in = rest[-2 - 2 * n_cast:-2 - n_cast]
    extra = rest[:-2 - 2 * n_cast]

    @pl.when(pl.program_id(1) == 0)
    def _():
        wb_ref[...] = w_ref[...].astype(BF16)

    rows = h_ref.shape[0] // PROJ_ROW_SLABS
    for m in range(PROJ_ROW_SLABS):
        sl = slice(m * rows, (m + 1) * rows)
        acc = jnp.dot(h_ref[sl, :], wb_ref[...], preferred_element_type=F32)
        if kind == "sigmoid":
            acc = jax.nn.sigmoid(acc)
        elif kind == "headnorm":
            acc = _head_norm(acc, extra[0][...])
        o_ref[sl, :] = acc.astype(o_ref.dtype)

    for src_ref, dst_ref in zip(cast_in, cast_out):
        dst_ref[...] = src_ref[...].astype(BF16)


def _proj(h, w, col0, n, kind, out_dtype, gain=None, casts=(), tm=1024, tn=1024):
    t, d = h.shape
    tm = _tile(t, tm)
    tn = _tile(n, tn)
    assert col0 % LANES == 0 and tn % LANES == 0 and tm % (PROJ_ROW_SLABS * 2 * SUBLANES) == 0
    n_i = t // tm
    n_steps = (n // tn) * n_i
    in_specs = [pl.BlockSpec((tm, d), lambda j, i: (i, 0)),
                pl.BlockSpec((pl.Element(d), pl.Element(tn)),
                             lambda j, i: (0, pl.multiple_of(col0 + j * tn, LANES)))]
    args = [h, w]
    if gain is not None:
        in_specs.append(pl.BlockSpec((1, tn), lambda j, i: (0, j)))
        args.append(gain)
    out_specs = [pl.BlockSpec((tm, tn), lambda j, i: (i, j))]
    out_shape = [jax.ShapeDtypeStruct((t, n), out_dtype)]
    for c in casts:
        rows = c.shape[0] // n_steps
        assert rows * n_steps == c.shape[0] and rows % (2 * SUBLANES) == 0, (c.shape, n_steps)
        spec = pl.BlockSpec((rows, c.shape[1]), lambda j, i: (j * n_i + i, 0))
        in_specs.append(spec)
        args.append(c)
        out_specs.append(spec)
        out_shape.append(jax.ShapeDtypeStruct(c.shape, BF16))
    return pl.pallas_call(
        functools.partial(_proj_kernel, kind=kind, n_cast=len(casts)),
        grid=(n // tn, n_i),
        in_specs=in_specs,
        out_specs=out_specs,
        out_shape=out_shape,
        scratch_shapes=[pltpu.VMEM((d, tn), BF16)],
        compiler_params=_cparams("parallel", "arbitrary"),
        name="proj_" + kind,
    )(*args)


BIG_VMEM_LIMIT = 60 * 1024 * 1024


def _proj_rnn_inputs_kernel(h_ref, w_ref, o_ref, wb_ref, *, row_slabs):
    @pl.when(pl.program_id(1) == 0)
    def _():
        wb_ref[...] = w_ref[...].astype(BF16)

    rows = h_ref.shape[0] // row_slabs

    def body(gelu):
        for m in range(row_slabs):
            sl = slice(m * rows, (m + 1) * rows)
            acc = jnp.dot(h_ref[sl, :], wb_ref[...], preferred_element_type=F32)
            o_ref[sl, :] = jax.nn.gelu(acc) if gelu else acc

    @pl.when(pl.program_id(0) == 0)
    def _():
        body(False)

    @pl.when(pl.program_id(0) == 1)
    def _():
        body(True)


def _proj_rnn_inputs(h, w, col0, d_rnn, tm=512, row_slabs=2):
    t, d = h.shape
    tm = _tile(t, tm)
    assert col0 % LANES == 0 and d_rnn % LANES == 0 and tm % (row_slabs * 2 * SUBLANES) == 0
    return pl.pallas_call(
        functools.partial(_proj_rnn_inputs_kernel, row_slabs=row_slabs),
        grid=(2, t // tm),
        in_specs=[pl.BlockSpec((tm, d), lambda j, i: (i, 0)),
                  pl.BlockSpec((pl.Element(d), pl.Element(d_rnn)),
                               lambda j, i: (0, pl.multiple_of(col0 + j * d_rnn, LANES)),
                               pipeline_mode=pl.Buffered(1))],
        out_specs=pl.BlockSpec((tm, d_rnn), lambda j, i: (i, j)),
        out_shape=jax.ShapeDtypeStruct((t, 2 * d_rnn), F32),
        scratch_shapes=[pltpu.VMEM((d, d_rnn), BF16)],
        compiler_params=pltpu.CompilerParams(dimension_semantics=("arbitrary", "arbitrary"),
                                             vmem_limit_bytes=BIG_VMEM_LIMIT),
        name="proj_rnn_inputs",
    )(h, w)


def _gate_window_starts(d_rnn):
    blk = d_rnn // N_RNN_BLOCKS
    n_tiles = d_rnn // LANES
    starts = []
    for j in range(n_tiles):
        b0 = (LANES * j) // blk
        b1 = (LANES * j + LANES - 1) // blk
        s = min((blk * b0) // LANES, n_tiles - GATE_WINDOW // LANES)
        assert s * LANES <= blk * b0 and blk * (b1 + 1) <= s * LANES + GATE_WINDOW
        starts.append(s)
    return starts


def _band_gate_weights(w_a, w_x):
    nb, blk, _ = w_a.shape
    d_rnn = nb * blk
    starts = _gate_window_starts(d_rnn)

    def band(w):
        tiles = []
        for j, s in enumerate(starts):
            c_lo, c_hi = LANES * j, LANES * (j + 1)
            pieces = []
            for b in range(c_lo // blk, (c_hi - 1) // blk + 1):
                oc0, oc1 = max(c_lo, blk * b), min(c_hi, blk * (b + 1))
                r0 = blk * b - s * LANES
                pieces.append(jnp.pad(w[b, :, oc0 - blk * b:oc1 - blk * b],
                                      ((r0, GATE_WINDOW - blk - r0), (0, 0))))
            tiles.append(jnp.concatenate(pieces, axis=1))
        return jnp.stack(tiles)

    return jnp.concatenate([band(w_a), band(w_x)], axis=-1).astype(BF16)


def _rnn_kernel(xr_ref, gg_ref, cw_ref, cb_ref, wband_ref, ba_ref, bx_ref, lam_ref, y_ref,
                tail_ref, xin_ref, hout_ref, xc_ref, xcb_ref, hcar_ref, *, tc, d_rnn, starts):
    n_tiles = d_rnn // LANES
    seg = tc // SUBLANES
    pitch = seg + SUBLANES

    @pl.when(pl.program_id(1) == 0)
    def _():
        tail_ref[...] = jnp.zeros_like(tail_ref)
        hcar_ref[...] = jnp.zeros_like(hcar_ref)

    sub = lax.broadcasted_iota(jnp.int32, (SUBLANES, LANES), 0)

    def seg_rows(p):
        return slice(p * seg, (p + 1) * seg)

    def slab_rows(p):
        return slice(p * pitch, p * pitch + seg)

    for j in range(n_tiles):
        cols = slice(j * LANES, (j + 1) * LANES)
        for p in range(SUBLANES):
            xin_ref[j, slab_rows(p), :] = xr_ref[seg_rows(p), cols]
        x = [xin_ref[j, pl.ds(t, SUBLANES, stride=pitch), :] for t in range(seg)]
        half_w = [0.5 * cw_ref[k:k + 1, cols] for k in range(CONV_WIDTH)]
        half_b = 0.5 * cb_ref[:, cols]

        def shifted(t, s):
            if t >= s:
                return x[t - s]
            i = SUBLANES - s + t
            prev = jnp.broadcast_to(tail_ref[i:i + 1, cols], (SUBLANES, LANES))
            return jnp.where(sub >= 1, pltpu.roll(x[seg + t - s], 1, axis=0), prev)

        xc = []
        for t in range(seg):
            acc = half_b + half_w[CONV_WIDTH - 1] * x[t]
            for s in range(1, CONV_WIDTH):
                acc = acc + half_w[CONV_WIDTH - 1 - s] * shifted(t, s)
            xc.append(acc)
        xc = jnp.concatenate(xc, axis=0)
        xc_ref[:, cols] = xc
        xcb_ref[:, cols] = xc.astype(BF16)

    tail_ref[...] = xr_ref[tc - SUBLANES:tc, :]

    for j in range(n_tiles):
        cols = slice(j * LANES, (j + 1) * LANES)
        ks = starts[j] * LANES
        pre = jnp.dot(xcb_ref[:, ks:ks + GATE_WINDOW], wband_ref[j], preferred_element_type=F32)
        tanh_r = jnp.tanh(pre[:, :LANES] + 0.5 * ba_ref[:, cols])
        tanh_i = jnp.tanh(pre[:, LANES:] + 0.5 * bx_ref[:, cols])
        nlam = -lam_ref[:, cols]
        softplus = jnp.maximum(nlam, 0.0) + jnp.log1p(jnp.exp(-jnp.abs(nlam)))
        half_c = (-0.5 * LRU_C) * softplus
        log_a = half_c * tanh_r + half_c
        a = jnp.exp(log_a)
        y = -jnp.tanh(log_a) * (a * a + 1.0)
        gated_x = (tanh_i + 1.0) * xc_ref[:, cols]
        u = y * lax.rsqrt(jnp.maximum(y, F32_TINY)) * gated_x
        a = a.reshape(seg, SUBLANES, LANES)
        u = u.reshape(seg, SUBLANES, LANES)

        h = u[0]
        decay = a[0]
        for t in range(1, seg):
            h = a[t] * h + u[t]
            decay = decay * a[t]
        for d in (1, 2, 4):
            m = sub >= d
            h = h + decay * jnp.where(m, pltpu.roll(h, d, axis=0), 0.0)
            decay = decay * jnp.where(m, pltpu.roll(decay, d, axis=0), 1.0)
        h_in = hcar_ref[:, cols]
        h_end = h + decay * h_in
        hcar_ref[:, cols] = jnp.broadcast_to(h_end[SUBLANES - 1:SUBLANES, :], (SUBLANES, LANES))
        h = jnp.where(sub >= 1, pltpu.roll(h_end, 1, axis=0), h_in)
        for t in range(seg):
            h = a[t] * h + u[t]
            hout_ref[j, pl.ds(t, SUBLANES, stride=pitch), :] = h
        for p in range(SUBLANES):
            y_ref[seg_rows(p), cols] = (gg_ref[seg_rows(p), cols] * hout_ref[j, slab_rows(p), :]).astype(y_ref.dtype)


def _rnn(xr_gg, conv_w, conv_b, wband, b_a, b_x, lam, tc=256):
    b, s, d_rnn = xr_gg.shape[0], xr_gg.shape[1], xr_gg.shape[2] // 2
    tc = _tile(s, tc)
    starts = _gate_window_starts(d_rnn)
    n_tiles = d_rnn // LANES
    slab = (n_tiles, tc + SUBLANES * SUBLANES, LANES)
    row = lambda v: v.reshape(1, d_rnn)
    full2 = lambda shape: pl.BlockSpec(shape, lambda bi, ci: (0, 0))
    chunk = pl.BlockSpec((None, tc, d_rnn), lambda bi, ci: (bi, ci, 0))
    gate_chunk = pl.BlockSpec((None, tc, d_rnn), lambda bi, ci: (bi, ci, 1))
    return pl.pallas_call(
        functools.partial(_rnn_kernel, tc=tc, d_rnn=d_rnn, starts=starts),
        grid=(b, s // tc),
        in_specs=[chunk, gate_chunk,
                  full2((CONV_WIDTH, d_rnn)), full2((1, d_rnn)),
                  pl.BlockSpec(wband.shape, lambda bi, ci: (0, 0, 0)),
                  full2((1, d_rnn)), full2((1, d_rnn)), full2((1, d_rnn))],
        out_specs=chunk,
        out_shape=jax.ShapeDtypeStruct((b, s, d_rnn), BF16),
        scratch_shapes=[pltpu.VMEM((SUBLANES, d_rnn), F32),
                        pltpu.VMEM(slab, F32),
                        pltpu.VMEM(slab, F32),
                        pltpu.VMEM((tc, d_rnn), F32),
                        pltpu.VMEM((tc, d_rnn), BF16),
                        pltpu.VMEM((SUBLANES, d_rnn), F32)],
        compiler_params=_cparams("parallel", "arbitrary"),
        name="rnn",
    )(xr_gg, xr_gg, conv_w, row(conv_b), wband, row(b_a), row(b_x), row(lam))


PRUNE_LOG = -104.0
SOFTPLUS_LINEAR = 80.0


def _attn_kernel(q_ref, k_ref, v_ref, tri_ref, o_ref, *, tq, hp):
    qi = pl.program_id(2)
    row = lax.broadcasted_iota(jnp.int32, (tq, tq), 0)
    col = lax.broadcasted_iota(jnp.int32, (tq, tq), 1)
    causal = col < row

    def sweep(last_kb, n_blocks, diag, state):
        rows = pl.ds(pl.multiple_of((last_kb - (n_blocks - 1)) * tq, tq), n_blocks * tq)
        zs, sps = [], []
        for hd in range(hp):
            lanes = slice(hd * HEAD_DIM, (hd + 1) * HEAD_DIM)
            z = lax.dot_general(q_ref[:, lanes], k_ref[rows, lanes], (((1,), (1,)), ((), ())),
                                preferred_element_type=F32)
            for blk in range(n_blocks):
                zb = z[:, blk * tq:(blk + 1) * tq]
                sp = jnp.where(zb > SOFTPLUS_LINEAR, zb, jnp.log(1.0 + jnp.exp(zb)))
                if diag and blk == n_blocks - 1:
                    sp = jnp.where(causal, sp, 0.0)
                zs.append(zb)
                sps.append(sp)
        excl = jnp.dot(jnp.concatenate([sp.astype(BF16) for sp in sps], axis=0), tri_ref[...],
                       preferred_element_type=F32)
        out = []
        for hd in range(hp):
            lanes = slice(hd * HEAD_DIM, (hd + 1) * HEAD_DIM)
            run, acc = state[hd]
            ws = [None] * n_blocks
            for blk in reversed(range(n_blocks)):
                u = hd * n_blocks + blk
                w = jnp.exp((zs[u] - sps[u]) + excl[u * tq:(u + 1) * tq] + run)
                if diag and blk == n_blocks - 1:
                    w = jnp.where(causal, w, 0.0)
                ws[blk] = w.astype(BF16)
                run = run - jnp.sum(sps[u], axis=-1, keepdims=True)
            acc = acc + jnp.dot(jnp.concatenate(ws, axis=1), v_ref[rows, lanes], preferred_element_type=F32)
            out.append((run, acc))
        return tuple(out)

    def run_max(state):
        m = jnp.max(state[0][0])
        for run, _ in state[1:]:
            m = jnp.maximum(m, jnp.max(run))
        return m

    zero = tuple((jnp.zeros((tq, 1), F32), jnp.zeros((tq, HEAD_DIM), F32)) for _ in range(hp))

    def store(state):
        for hd, (_, acc) in enumerate(state):
            o_ref[:, hd * HEAD_DIM:(hd + 1) * HEAD_DIM] = acc.astype(o_ref.dtype)

    @pl.when(qi == 0)
    def _():
        store(sweep(qi, 1, True, zero))

    @pl.when(qi > 0)
    def _():
        state = sweep(qi, 2, True, zero)

        def cond(c):
            kb, rmax, _ = c
            return jnp.logical_and(kb >= 0, rmax > PRUNE_LOG)

        def body(c):
            kb, _, st = c
            st = sweep(kb, 1, False, st)
            return kb - 1, run_max(st), st

        _, _, state = lax.while_loop(cond, body, (qi - 2, run_max(state), state))
        store(state)


def _attention(qk, v, tq=256, hp=8):
    b, s, _ = v.shape
    tq = _tile(s, tq)
    hw = hp * HEAD_DIM
    n_hg = N_HEADS // hp
    tri = jnp.asarray(-np.tril(np.ones((tq, tq), np.float32), -1), BF16)
    return pl.pallas_call(
        functools.partial(_attn_kernel, tq=tq, hp=hp),
        grid=(b, n_hg, s // tq),
        in_specs=[pl.BlockSpec((None, tq, hw), lambda bi, h, qi: (bi, qi, h)),
                  pl.BlockSpec((None, s, hw), lambda bi, h, qi: (bi, 0, n_hg + h)),
                  pl.BlockSpec((None, s, hw), lambda bi, h, qi: (bi, 0, h)),
                  pl.BlockSpec((tq, tq), lambda bi, h, qi: (0, 0))],
        out_specs=pl.BlockSpec((None, tq, hw), lambda bi, h, qi: (bi, qi, h)),
        out_shape=jax.ShapeDtypeStruct(v.shape, BF16),
        compiler_params=_cparams("parallel", "parallel", "arbitrary"),
        name="attention",
    )(qk, qk, v, tri)


def _mix_kernel(yr_ref, oa_ref, wr_ref, wa_ref, sr_ref, sa_ref, o_ref, *, row_slabs):
    rows = yr_ref.shape[0] // row_slabs
    for m in range(row_slabs):
        sl = slice(m * rows, (m + 1) * rows)
        ya = jnp.dot(yr_ref[sl, :], wr_ref[...], preferred_element_type=F32)
        yb = jnp.dot(oa_ref[sl, :], wa_ref[...], preferred_element_type=F32)
        o_ref[sl, :] = (sr_ref[sl, :] * ya + sa_ref[sl, :] * yb).astype(o_ref.dtype)


def _mix(yr, oa, w_r, w_a, gates, tm=512, row_slabs=2):
    t, d_rnn = yr.shape
    d_attn = oa.shape[1]
    n = w_r.shape[1]
    tm = _tile(t, tm)
    once = dict(pipeline_mode=pl.Buffered(1))
    return pl.pallas_call(
        functools.partial(_mix_kernel, row_slabs=row_slabs),
        grid=(t // tm,),
        in_specs=[pl.BlockSpec((tm, d_rnn), lambda i: (i, 0)),
                  pl.BlockSpec((tm, d_attn), lambda i: (i, 0)),
                  pl.BlockSpec((d_rnn, n), lambda i: (0, 0), **once),
                  pl.BlockSpec((d_attn, n), lambda i: (0, 0), **once),
                  pl.BlockSpec((tm, n), lambda i: (i, 0)),
                  pl.BlockSpec((tm, n), lambda i: (i, 1))],
        out_specs=pl.BlockSpec((tm, n), lambda i: (i, 0)),
        out_shape=jax.ShapeDtypeStruct((t, n), BF16),
        compiler_params=_cparams("parallel"),
        name="mix",
    )(yr, oa, w_r, w_a, gates, gates)


def _oproj_kernel(m_ref, w_ref, x_ref, g_ref, x1_ref, h_ref):
    x1 = x_ref[...] + jnp.dot(m_ref[...], w_ref[...], preferred_element_type=F32)
    x1_ref[...] = x1
    ms = jnp.mean(x1 * x1, axis=-1, keepdims=True)
    h_ref[...] = (x1 * lax.rsqrt(ms + EPS) * g_ref[...]).astype(h_ref.dtype)


def _oproj(mix, w_o, x, g_next, tm=512):
    t, d = x.shape
    tm = _tile(t, tm)
    return pl.pallas_call(
        _oproj_kernel,
        grid=(t // tm,),
        in_specs=[pl.BlockSpec((tm, d), lambda i: (i, 0)),
                  pl.BlockSpec((d, d), lambda i: (0, 0)),
                  pl.BlockSpec((tm, d), lambda i: (i, 0)),
                  pl.BlockSpec((1, d), lambda i: (0, 0))],
        out_specs=[pl.BlockSpec((tm, d), lambda i: (i, 0)),
                   pl.BlockSpec((tm, d), lambda i: (i, 0))],
        out_shape=[jax.ShapeDtypeStruct((t, d), F32), jax.ShapeDtypeStruct((t, d), BF16)],
        compiler_params=_cparams("parallel"),
        name="oproj",
    )(mix, w_o, x, g_next.reshape(1, d))


def _ffn_kernel(h_ref, wg_ref, wu_ref, wd_ref, x_ref, x2_ref):
    f = pl.program_id(1)

    @pl.when(f == 0)
    def _():
        x2_ref[...] = x_ref[...]

    h = h_ref[...]
    g = jnp.dot(h, wg_ref[...], preferred_element_type=F32)
    u = jnp.dot(h, wu_ref[...], preferred_element_type=F32)
    act = (jax.nn.silu(g) * u).astype(BF16)
    x2_ref[...] += jnp.dot(act, wd_ref[...], preferred_element_type=F32)


def _ffn(h2, w_gu, w_d, x1, tm=1024, tf=512):
    t, d = x1.shape
    d_ff = w_d.shape[0]
    tm = _tile(t, tm)
    tf = _tile(d_ff, tf)
    nf = d_ff // tf
    return pl.pallas_call(
        _ffn_kernel,
        grid=(t // tm, d_ff // tf),
        in_specs=[pl.BlockSpec((tm, d), lambda i, f: (i, 0)),
                  pl.BlockSpec((d, tf), lambda i, f: (0, f)),
                  pl.BlockSpec((d, tf), lambda i, f: (0, nf + f)),
                  pl.BlockSpec((tf, d), lambda i, f: (f, 0)),
                  pl.BlockSpec((tm, d), lambda i, f: (i, 0))],
        out_specs=pl.BlockSpec((tm, d), lambda i, f: (i, 0), pipeline_mode=pl.Buffered(1)),
        out_shape=jax.ShapeDtypeStruct((t, d), F32),
        compiler_params=pltpu.CompilerParams(dimension_semantics=("parallel", "arbitrary"),
                                             vmem_limit_bytes=BIG_VMEM_LIMIT),
        name="ffn",
    )(h2, w_gu, w_gu, w_d, x1)


def _ple_kernel(x_ref, g_ref, wg_ref, p_ref, wp_ref, o_ref, *, row_slabs):
    rows = x_ref.shape[0] // row_slabs
    for m in range(row_slabs):
        sl = slice(m * rows, (m + 1) * rows)
        x = x_ref[sl, :]
        ms = jnp.mean(x * x, axis=-1, keepdims=True)
        h = (x * lax.rsqrt(ms + EPS) * g_ref[...]).astype(BF16)
        gate = jax.nn.sigmoid(jnp.dot(h, wg_ref[...], preferred_element_type=F32))
        emb = jnp.dot(p_ref[sl, :].astype(BF16), wp_ref[...].astype(BF16), preferred_element_type=F32)
        o_ref[sl, :] = x + gate * emb


def _ple(x2, g, w_gate, p, w_proj, tm=512, row_slabs=2):
    t, d = x2.shape
    dp = p.shape[1]
    tm = _tile(t, tm)
    once = dict(pipeline_mode=pl.Buffered(1))
    return pl.pallas_call(
        functools.partial(_ple_kernel, row_slabs=row_slabs),
        grid=(t // tm,),
        in_specs=[pl.BlockSpec((tm, d), lambda i: (i, 0)),
                  pl.BlockSpec((1, d), lambda i: (0, 0)),
                  pl.BlockSpec((d, d), lambda i: (0, 0), **once),
                  pl.BlockSpec((tm, dp), lambda i: (i, 0)),
                  pl.BlockSpec((dp, d), lambda i: (0, 0), **once)],
        out_specs=pl.BlockSpec((tm, d), lambda i: (i, 0)),
        out_shape=jax.ShapeDtypeStruct((t, d), F32),
        compiler_params=_cparams("parallel"),
        name="ple",
    )(x2, g.reshape(1, d), w_gate, p, w_proj)


def _layer(x, p, g_mix, w_in, conv_w, conv_b, w_rg_a, b_rg_a, w_rg_x, b_rg_x, lru_lambda, q_gain,
           k_gain, w_rnn_out, w_attn_out, w_o, g_ffn, w_ffn_gu, w_ffn_down, g_ple, w_ple_gate,
           w_ple_proj):
    b, s, d = x.shape
    t = b * s
    d_rnn = w_rnn_out.shape[0]
    d_attn = w_attn_out.shape[0]
    x2d = x.reshape(t, d)

    o0 = 0
    o1 = o0 + d_rnn
    o2 = o1 + d_rnn
    o3 = o2 + 2 * d_attn
    o4 = o3 + d_attn
    o5 = o4 + 2 * d
    assert o5 == w_in.shape[1]

    h = _rmsnorm(x2d, g_mix)
    xr_gg = _proj_rnn_inputs(h, w_in, o0, d_rnn)
    qk_gain = jnp.concatenate([jnp.tile(q_gain * (1.0 / math.sqrt(HEAD_DIM)), N_HEADS),
                               jnp.tile(k_gain, N_HEADS)]).reshape(1, 2 * d_attn)
    qk, w_gu_b, w_attn_out_b = _proj(h, w_in, o2, 2 * d_attn, "headnorm", BF16, gain=qk_gain,
                                     casts=(w_ffn_gu, w_attn_out))
    v, w_rnn_out_b = _proj(h, w_in, o3, d_attn, "none", BF16, casts=(w_rnn_out,))
    gates, w_down_b, w_o_b, w_ple_gate_b = _proj(h, w_in, o4, 2 * d, "sigmoid", BF16,
                                                 casts=(w_ffn_down, w_o, w_ple_gate))

    wband = _band_gate_weights(w_rg_a, w_rg_x)
    yr = _rnn(xr_gg.reshape(b, s, 2 * d_rnn), conv_w, conv_b, wband, b_rg_a, b_rg_x, lru_lambda)
    oa = _attention(qk.reshape(b, s, 2 * d_attn), v.reshape(b, s, d_attn))

    mix = _mix(yr.reshape(t, d_rnn), oa.reshape(t, d_attn), w_rnn_out_b, w_attn_out_b, gates)
    x1, h2 = _oproj(mix, w_o_b, x2d, g_ffn)
    x2 = _ffn(h2, w_gu_b, w_down_b, x1)
    out = _ple(x2, g_ple, w_ple_gate_b, p.reshape(t, -1), w_ple_proj)
    return out.reshape(b, s, d)


def kernel(x, p, g_mix, w_in, conv_w, conv_b, w_rg_a, b_rg_a, w_rg_x, b_rg_x, lru_lambda, q_gain,
           k_gain, w_rnn_out, w_attn_out, w_o, g_ffn, w_ffn_gu, w_ffn_down, g_ple, w_ple_gate,
           w_ple_proj):
    params = (g_mix, w_in, conv_w, conv_b, w_rg_a, b_rg_a, w_rg_x, b_rg_x, lru_lambda, q_gain,
              k_gain, w_rnn_out, w_attn_out, w_o, g_ffn, w_ffn_gu, w_ffn_down, g_ple, w_ple_gate,
              w_ple_proj)
    for i in range(p.shape[0]):
        x = _layer(x, p[i], *[a[i] for a in params])
    return x
```
